```python
import jax, jax.numpy as jnp
from jax import lax
import numpy as np

D_MODEL = 1024
BATCH = 8
SEQ = 2048
DEPTH = 4
DEC_BATCH = 128
DEC_SEQ = 4
PAST_LEN = 8192
PAGE_SIZE = 128

N_AB = (DEPTH + 1) // 2
N_C = DEPTH // 2
GLA_HEADS = 4
GLA_DK = 64
GLA_DV = 128
GLA_RANK = 16
GLA_TAU = 16.0
GLA_CHUNK = 64
MLA_HEADS = 8
MLA_Q_LORA = 256
MLA_KV_LORA = 256
MLA_NOPE = 64
MLA_ROPE = 32
MLA_V = 64
MLA_SCALE = (MLA_NOPE + MLA_ROPE) ** -0.5
ROPE_THETA = 10000.0
Q_BLOCK = 128
CONV_W = 31
D_FF = 4 * D_MODEL
EPS = 1e-6

GLA_QK_W = GLA_HEADS * GLA_DK
GLA_V_W = GLA_HEADS * GLA_DV
MLA_OUT_W = MLA_HEADS * MLA_V
D_MIX_OUT = GLA_V_W + MLA_OUT_W
SPLIT_POINTS = (GLA_QK_W,
                2 * GLA_QK_W,
                2 * GLA_QK_W + GLA_V_W,
                2 * GLA_QK_W + GLA_V_W + GLA_RANK,
                2 * GLA_QK_W + 2 * GLA_V_W + GLA_RANK,
                2 * GLA_QK_W + 2 * GLA_V_W + GLA_RANK + MLA_Q_LORA)
D_IN = SPLIT_POINTS[-1] + MLA_KV_LORA + MLA_ROPE

kernel_name = 'hybrid_gla_mla_conformer_decode_step'


def rmsnorm(x, g):
    xf = x.astype(jnp.float32)
    y = xf * lax.rsqrt(jnp.mean(xf * xf, axis=-1, keepdims=True) + EPS)
    return (y * g.astype(jnp.float32)).astype(x.dtype)


def layernorm(x, g, b):
    xf = x.astype(jnp.float32)
    mu = jnp.mean(xf, axis=-1, keepdims=True)
    var = jnp.mean(jnp.square(xf - mu), axis=-1, keepdims=True)
    y = (xf - mu) * lax.rsqrt(var + EPS)
    return (y * g.astype(jnp.float32) + b.astype(jnp.float32)).astype(x.dtype)


def rope_table(pos):
    inv = jnp.power(ROPE_THETA, -jnp.arange(0, MLA_ROPE, 2, dtype=jnp.float32) / MLA_ROPE)
    ang = pos.astype(jnp.float32)[:, None] * inv[None, :]
    ang = jnp.concatenate([ang, ang], axis=-1)
    return jnp.cos(ang), jnp.sin(ang)


def apply_rope(x, cos, sin):
    half = x.shape[-1] // 2
    rot = jnp.concatenate([-x[..., half:], x[..., :half]], axis=-1)
    return (x * cos + rot * sin).astype(x.dtype)


def gla_recurrence(q, k, v, log_a, s0):
    f32 = jnp.float32
    B, T = q.shape[0], q.shape[1]
    c = min(GLA_CHUNK, T)
    n = -(-T // c)
    pad = n * c - T

    def prep(z):
        z = jnp.pad(z.astype(f32), ((0, 0), (0, pad), (0, 0), (0, 0)))
        return z.reshape(B, n, c, z.shape[2], z.shape[3]).transpose(1, 0, 3, 2, 4)

    qc, kc, vc, gc = prep(q), prep(k), prep(v), prep(log_a)
    causal = jnp.tril(jnp.ones((c, c), dtype=bool))[:, :, None]

    def step(s, inp):
        qi, ki, vi, gi = inp
        b = jnp.cumsum(gi, axis=2)
        o_inter = jnp.einsum('bhtd,bhde->bhte', qi * jnp.exp(b), s)
        diff = b[:, :, :, None, :] - b[:, :, None, :, :]
        decay = jnp.exp(jnp.where(causal, diff, -jnp.inf))
        attn = jnp.einsum('bhtd,bhsd,bhtsd->bhts', qi, ki, decay)
        o = o_inter + jnp.einsum('bhts,bhse->bhte', attn, vi)
        b_last = b[:, :, -1:, :]
        s_new = (jnp.exp(b_last[:, :, 0, :])[..., None] * s
                 + jnp.einsum('bhsd,bhse->bhde', ki * jnp.exp(b_last - b), vi))
        return s_new, o

    s_final, oc = lax.scan(step, s0.astype(f32), (qc, kc, vc, gc))
    o = oc.transpose(1, 0, 3, 2, 4).reshape(B, n * c, q.shape[2], v.shape[3])[:, :T]
    return o, s_final


def mla_self(q_nope, q_pe, c_kv, k_pe, w_ukv):
    B, S = c_kv.shape[0], c_kv.shape[1]
    kv = (c_kv @ w_ukv).reshape(B, S, MLA_HEADS, MLA_NOPE + MLA_V)
    k_nope, v = kv[..., :MLA_NOPE], kv[..., MLA_NOPE:]
    qb = min(Q_BLOCK, S)
    nb = S // qb
    key_pos = jnp.arange(S)

    def block(i):
        start = i * qb
        qn = lax.dynamic_slice_in_dim(q_nope, start, qb, axis=1)
        qp = lax.dynamic_slice_in_dim(q_pe, start, qb, axis=1)
        sc = jnp.einsum('bthn,bshn->bhts', qn, k_nope) + jnp.einsum('bthr,bsr->bhts', qp, k_pe)
        sc = sc.astype(jnp.float32) * MLA_SCALE
        mask = (start + jnp.arange(qb))[:, None] >= key_pos[None, :]
        p = jax.nn.softmax(jnp.where(mask, sc, -jnp.inf), axis=-1).astype(v.dtype)
        return jnp.einsum('bhts,bshv->bthv', p, v)

    o = lax.map(block, jnp.arange(nb))
    return o.transpose(1, 0, 2, 3, 4).reshape(B, S, MLA_HEADS * MLA_V)


def mla_cached(q_nope, q_pe, c_new, kr_new, past_c, past_r, w_ukv):
    B, T = c_new.shape[0], c_new.shape[1]
    P = past_c.shape[1]
    w = w_ukv.reshape(MLA_KV_LORA, MLA_HEADS, MLA_NOPE + MLA_V)
    w_uk, w_uv = w[..., :MLA_NOPE], w[..., MLA_NOPE:]
    q_lat = jnp.einsum('bthn,chn->bthc', q_nope, w_uk)
    s_past = jnp.einsum('bthc,bpc->bhtp', q_lat, past_c) + jnp.einsum('bthr,bpr->bhtp', q_pe, past_r)
    s_new = jnp.einsum('bthc,bsc->bhts', q_lat, c_new) + jnp.einsum('bthr,bsr->bhts', q_pe, kr_new)
    causal = jnp.tril(jnp.ones((T, T), dtype=bool))
    s_new = jnp.where(causal, s_new.astype(jnp.float32), -jnp.inf)
    sc = jnp.concatenate([s_past.astype(jnp.float32), s_new], axis=-1) * MLA_SCALE
    p = jax.nn.softmax(sc, axis=-1).astype(c_new.dtype)
    o_lat = (jnp.einsum('bhtp,bpc->bthc', p[..., :P], past_c)
             + jnp.einsum('bhts,bsc->bthc', p[..., P:], c_new))
    o = jnp.einsum('bthc,chv->bthv', o_lat, w_uv)
    return o.reshape(B, T, MLA_HEADS * MLA_V)


def ab_mixer(h, gla_s0, cos, sin, past_c, past_r, w_in, w_gate_a2, b_gate_a, gla_norm,
             q_norm, kv_norm, w_uq, w_ukv, w_out):
    B, T = h.shape[0], h.shape[1]
    z = h @ w_in
    q, k, v, a, g, dq, dkv = jnp.split(z, SPLIT_POINTS, axis=-1)
    log_a = jax.nn.log_sigmoid((a @ w_gate_a2 + b_gate_a).astype(jnp.float32)) / GLA_TAU
    qh = q.reshape(B, T, GLA_HEADS, GLA_DK) * (GLA_DK ** -0.5)
    kh = k.reshape(B, T, GLA_HEADS, GLA_DK)
    vh = v.reshape(B, T, GLA_HEADS, GLA_DV)
    o, s_new = gla_recurrence(qh, kh, vh, log_a.reshape(B, T, GLA_HEADS, GLA_DK), gla_s0)
    o = rmsnorm(o.astype(h.dtype), gla_norm) * jax.nn.silu(g.reshape(B, T, GLA_HEADS, GLA_DV))
    o_gla = o.reshape(B, T, GLA_V_W)
    cq = rmsnorm(dq, q_norm)
    qm = (cq @ w_uq).reshape(B, T, MLA_HEADS, MLA_NOPE + MLA_ROPE)
    q_nope = qm[..., :MLA_NOPE]
    q_pe = apply_rope(qm[..., MLA_NOPE:], cos[:, None, :], sin[:, None, :])
    c_new = rmsnorm(dkv[..., :MLA_KV_LORA], kv_norm)
    kr_new = apply_rope(dkv[..., MLA_KV_LORA:], cos, sin)
    if past_c is None:
        o_mla = mla_self(q_nope, q_pe, c_new, kr_new, w_ukv)
    else:
        o_mla = mla_cached(q_nope, q_pe, c_new, kr_new, past_c, past_r, w_ukv)
    out = jnp.concatenate([o_gla, o_mla], axis=-1) @ w_out
    return out, s_new.astype(h.dtype), c_new, kr_new


def conv_module(h, buf, w_pw1, b_pw1, w_dw, b_dw, ln_g, ln_b, w_pw2, b_pw2):
    u = h @ w_pw1 + b_pw1
    u = u[..., :D_MODEL] * jax.nn.sigmoid(u[..., D_MODEL:])
    ext = jnp.concatenate([buf.astype(u.dtype), u], axis=1)
    y = lax.conv_general_dilated(ext, w_dw[:, None, :].astype(ext.dtype), window_strides=(1,),
                                 padding='VALID', dimension_numbers=('NWC', 'WIO', 'NWC'),
                                 feature_group_count=D_MODEL) + b_dw
    y = jax.nn.silu(layernorm(y, ln_g, ln_b))
    return y @ w_pw2 + b_pw2, ext[:, -(CONV_W - 1):]


def sq_relu_mlp(h, w_up, w_down):
    return jnp.square(jax.nn.relu(h @ w_up)) @ w_down


def setup_inputs(seed: int = 0) -> dict:
    key = jax.random.key(seed)
    ks = jax.random.split(key, 32)
    n_pages = PAST_LEN // PAGE_SIZE
    n_used = DEC_BATCH * n_pages
    n_pool = n_used + n_used // 4
    f32 = jnp.float32

    def nrm(k, shape, scale):
        return jax.random.normal(k, shape, f32) * scale

    def gain(k, shape):
        return 1.0 + 0.02 * jax.random.normal(k, shape, f32)

    page_table = jax.random.permutation(ks[6], n_pool)[:n_used].reshape(DEC_BATCH, n_pages).astype(jnp.int32)
    return {
        'x_prompt': nrm(ks[0], (BATCH, SEQ, D_MODEL), 1.0),
        'x_sample': nrm(ks[1], (DEC_BATCH, DEC_SEQ, D_MODEL), 1.0),
        'cache_kv': nrm(ks[2], (N_AB, n_pool, PAGE_SIZE, MLA_KV_LORA), 1.0),
        'cache_kr': nrm(ks[3], (N_AB, n_pool, PAGE_SIZE, MLA_ROPE), 1.0),
        'state_gla': nrm(ks[4], (N_AB, DEC_BATCH, GLA_HEADS, GLA_DK, GLA_DV), 0.5),
        'state_conv': nrm(ks[5], (N_C, DEC_BATCH, CONV_W - 1, D_MODEL), 0.5),
        'page_table': page_table,
        'norm_mix': gain(ks[7], (DEPTH, D_MODEL)),
        'norm_mlp': gain(ks[8], (DEPTH, D_MODEL)),
        'norm_final': gain(ks[9], (D_MODEL,)),
        'w_in': nrm(ks[10], (N_AB, D_MODEL, D_IN), D_MODEL ** -0.5),
        'w_gate_a2': nrm(ks[11], (N_AB, GLA_RANK, GLA_QK_W), GLA_RANK ** -0.5),
        'b_gate_a': nrm(ks[12], (N_AB, GLA_QK_W), 0.1),
        'gla_norm': gain(ks[13], (N_AB, GLA_DV)),
        'mla_q_norm': gain(ks[14], (N_AB, MLA_Q_LORA)),
        'mla_kv_norm': gain(ks[15], (N_AB, MLA_KV_LORA)),
        'w_uq': nrm(ks[16], (N_AB, MLA_Q_LORA, MLA_HEADS * (MLA_NOPE + MLA_ROPE)), MLA_Q_LORA ** -0.5),
        'w_ukv': nrm(ks[17], (N_AB, MLA_KV_LORA, MLA_HEADS * (MLA_NOPE + MLA_V)), MLA_KV_LORA ** -0.5),
        'w_out_ab': nrm(ks[18], (N_AB, D_MIX_OUT, D_MODEL), D_MIX_OUT ** -0.5),
        'w_pw1': nrm(ks[19], (N_C, D_MODEL, 2 * D_MODEL), D_MODEL ** -0.5),
        'b_pw1': nrm(ks[20], (N_C, 2 * D_MODEL), 0.02),
        'w_dw': nrm(ks[21], (N_C, CONV_W, D_MODEL), CONV_W ** -0.5),
        'b_dw': nrm(ks[22], (N_C, D_MODEL), 0.02),
        'conv_ln_g': gain(ks[23], (N_C, D_MODEL)),
        'conv_ln_b': nrm(ks[24], (N_C, D_MODEL), 0.02),
        'w_pw2': nrm(ks[25], (N_C, D_MODEL, D_MODEL), D_MODEL ** -0.5),
        'b_pw2': nrm(ks[26], (N_C, D_MODEL), 0.02),
        'w_up': nrm(ks[27], (DEPTH, D_MODEL, D_FF), D_MODEL ** -0.5),
        'w_down': nrm(ks[28], (DEPTH, D_FF, D_MODEL), 0.5 * D_FF ** -0.5),
    }


def reference(x_prompt, x_sample, cache_kv, cache_kr, state_gla, state_conv, page_table,
              norm_mix, norm_mlp, norm_final, w_in, w_gate_a2, b_gate_a, gla_norm,
              mla_q_norm, mla_kv_norm, w_uq, w_ukv, w_out_ab, w_pw1, b_pw1, w_dw, b_dw,
              conv_ln_g, conv_ln_b, w_pw2, b_pw2, w_up, w_down):
    Bp, S = x_prompt.shape[0], x_prompt.shape[1]
    Bd, T = x_sample.shape[0], x_sample.shape[1]
    past = page_table.shape[1] * PAGE_SIZE
    cos_p, sin_p = rope_table(jnp.arange(S))
    cos_s, sin_s = rope_table(past + jnp.arange(T))
    xp, xs = x_prompt, x_sample
    kv_p, kr_p, gla_p, conv_p = [], [], [], []
    kv_s, kr_s, gla_s, conv_s = [], [], [], []
    for l in range(DEPTH):
        i = l // 2
        hp = rmsnorm(xp, norm_mix[l])
        hs = rmsnorm(xs, norm_mix[l])
        if l % 2 == 0:
            ab_w = (w_in[i], w_gate_a2[i], b_gate_a[i], gla_norm[i], mla_q_norm[i],
                    mla_kv_norm[i], w_uq[i], w_ukv[i], w_out_ab[i])
            s0_p = jnp.zeros((Bp, GLA_HEADS, GLA_DK, GLA_DV), jnp.float32)
            mp, sp, cp, rp = ab_mixer(hp, s0_p, cos_p, sin_p, None, None, *ab_w)
            past_c = cache_kv[i][page_table].reshape(Bd, past, MLA_KV_LORA)
            past_r = cache_kr[i][page_table].reshape(Bd, past, MLA_ROPE)
            ms, ss, cs, rs = ab_mixer(hs, state_gla[i], cos_s, sin_s, past_c, past_r, *ab_w)
            kv_p.append(cp); kr_p.append(rp); gla_p.append(sp)
            kv_s.append(cs); kr_s.append(rs); gla_s.append(ss)
        else:
            c_w = (w_pw1[i], b_pw1[i], w_dw[i], b_dw[i], conv_ln_g[i], conv_ln_b[i], w_pw2[i], b_pw2[i])
            buf_p = jnp.zeros((Bp, CONV_W - 1, D_MODEL), xp.dtype)
            mp, bp = conv_module(hp, buf_p, *c_w)
            ms, bs = conv_module(hs, state_conv[i], *c_w)
            conv_p.append(bp); conv_s.append(bs)
        xp = xp + mp
        xs = xs + ms
        xp = xp + sq_relu_mlp(rmsnorm(xp, norm_mlp[l]), w_up[l], w_down[l])
        xs = xs + sq_relu_mlp(rmsnorm(xs, norm_mlp[l]), w_up[l], w_down[l])
    y_prompt = rmsnorm(xp, norm_final)
    y_sample = rmsnorm(xs, norm_final)
    return (y_prompt, y_sample,
            jnp.stack(kv_p), jnp.stack(kr_p), jnp.stack(gla_p), jnp.stack(conv_p),
            jnp.stack(kv_s), jnp.stack(kr_s), jnp.stack(gla_s), jnp.stack(conv_s))
```

```python
import functools
import math

import jax
import jax.numpy as jnp
from jax import lax
from jax.experimental import pallas as pl
from jax.experimental.pallas import tpu as pltpu

F32 = jnp.float32
BF16 = jnp.bfloat16

GLA_HEADS = 4
GLA_DK = 64
GLA_DV = 128
GLA_RANK = 16
GLA_TAU = 16.0
GLA_CHUNK = 64
GLA_SUB = 16
MLA_HEADS = 8
MLA_NOPE = 64
MLA_ROPE = 32
MLA_V = 64
MLA_SCALE = (MLA_NOPE + MLA_ROPE) ** -0.5
ROPE_THETA = 10000.0
EPS = 1e-6

LANES = 128
HEAD_PAD = 128
ROPE_LO = MLA_NOPE
ROPE_HALF = MLA_ROPE // 2
VMEM_LIMIT = 56 * 1024 * 1024


def _cparams(sem):
    return pltpu.CompilerParams(dimension_semantics=sem, vmem_limit_bytes=VMEM_LIMIT)


def _row_tile(n, target):
    best = None
    for t in range(8, min(n, target) + 1, 8):
        if n % t == 0:
            best = t
    assert best is not None, n
    return best


def _const_spec(shape):
    nd = len(shape)
    return pl.BlockSpec(shape, lambda *_: (0,) * nd, pipeline_mode=pl.Buffered(1))


def _rms(x, g):
    ms = jnp.mean(x * x, axis=-1, keepdims=True)
    return x * lax.rsqrt(ms + EPS) * g


def _sigmoid(x):
    return 1.0 / (1.0 + jnp.exp(-x))


def _dot(a, b):
    return jnp.dot(a, b, preferred_element_type=F32)


def _dot_nt(a, b):
    return lax.dot_general(a, b, (((1,), (1,)), ((), ())), preferred_element_type=F32)


def _norm_matmul_kernel(x_ref, g_ref, w_ref, b_ref, o_ref, *, glu):
    h = _rms(x_ref[...], g_ref[...]).astype(BF16)
    u = _dot(h, w_ref[...]) + b_ref[...]
    if glu:
        half = u.shape[1] // 2
        u = u[:, :half] * _sigmoid(u[:, half:])
    o_ref[...] = u


def norm_matmul(x, g, w_bf16, bias, *, glu=False, name):
    r, d = x.shape
    n = w_bf16.shape[1]
    n_out = n // 2 if glu else n
    tm = _row_tile(r, 512)
    return pl.pallas_call(
        functools.partial(_norm_matmul_kernel, glu=glu),
        grid=(r // tm,),
        in_specs=[
            pl.BlockSpec((tm, d), lambda i: (i, 0)),
            _const_spec((1, d)),
            _const_spec((d, n)),
            _const_spec((1, n)),
        ],
        out_specs=pl.BlockSpec((tm, n_out), lambda i: (i, 0)),
        out_shape=jax.ShapeDtypeStruct((r, n_out), F32),
        compiler_params=_cparams(("parallel",)),
        name=name,
    )(x, g.reshape(1, d), w_bf16, bias.reshape(1, n))


def _mlp_kernel(x_ref, g_ref, wup_ref, wdn_ref, o_ref, h_ref):
    x = x_ref[...]
    h_ref[...] = _rms(x, g_ref[...]).astype(BF16)
    o_ref[...] = x

    def body(f, carry):
        u = jnp.maximum(_dot(h_ref[...], wup_ref[f]), 0.0)
        o_ref[...] += _dot((u * u).astype(BF16), wdn_ref[f])
        return carry

    lax.fori_loop(0, wup_ref.shape[0], body, 0)


def mlp_residual(x, g, wup3, wdn3, *, name):
    r, d = x.shape
    nf, _, tf = wup3.shape
    tm = _row_tile(r, 512)
    return pl.pallas_call(
        _mlp_kernel,
        grid=(r // tm,),
        in_specs=[
            pl.BlockSpec((tm, d), lambda i: (i, 0)),
            _const_spec((1, d)),
            _const_spec((nf, d, tf)),
            _const_spec((nf, tf, d)),
        ],
        out_specs=pl.BlockSpec((tm, d), lambda i: (i, 0)),
        out_shape=jax.ShapeDtypeStruct((r, d), F32),
        scratch_shapes=[pltpu.VMEM((tm, d), BF16)],
        compiler_params=_cparams(("parallel",)),
        name=name,
    )(x, g.reshape(1, d), wup3, wdn3)


def _out_proj_kernel(a_ref, w_ref, r_ref, o_ref):
    o_ref[...] = r_ref[...] + _dot(a_ref[...], w_ref[...])


def out_proj_residual(a_bf16, w_bf16, res, *, name):
    r, k = a_bf16.shape
    d = w_bf16.shape[1]
    tm = _row_tile(r, 512)
    return pl.pallas_call(
        _out_proj_kernel,
        grid=(r // tm,),
        in_specs=[
            pl.BlockSpec((tm, k), lambda i: (i, 0)),
            _const_spec((k, d)),
            pl.BlockSpec((tm, d), lambda i: (i, 0)),
        ],
        out_specs=pl.BlockSpec((tm, d), lambda i: (i, 0)),
        out_shape=jax.ShapeDtypeStruct((r, d), F32),
        compiler_params=_cparams(("parallel",)),
        name=name,
    )(a_bf16, w_bf16, res)


def _final_norm_kernel(x_ref, g_ref, o_ref):
    o_ref[...] = _rms(x_ref[...], g_ref[...])


def final_norm(x, g, row0, rows, *, name):
    d = x.shape[1]
    tm = _row_tile(math.gcd(row0, rows) if row0 else rows, 512)
    off = row0 // tm
    return pl.pallas_call(
        _final_norm_kernel,
        grid=(rows // tm,),
        in_specs=[pl.BlockSpec((tm, d), lambda i: (i + off, 0)), _const_spec((1, d))],
        out_specs=pl.BlockSpec((tm, d), lambda i: (i, 0)),
        out_shape=jax.ShapeDtypeStruct((rows, d), F32),
        compiler_params=_cparams(("parallel",)),
        name=name,
    )(x, g.reshape(1, d))


def _rope_block(x, cos_t, sin_a, sin_b):
    up = pltpu.roll(x, LANES - ROPE_HALF, 1)
    dn = pltpu.roll(x, ROPE_HALF, 1)
    return x * cos_t + up * sin_a + dn * sin_b


def _ab_proj_kernel(dq_ref, dkv_ref, akr_ref, cos_ref, sa_ref, sb_ref, qn_ref, kvn_ref,
                    wuq_ref, w2_ref, b2_ref, loga_ref, q_ref, c_ref, kr_ref, kpe_ref):
    cos_t, sin_a, sin_b = cos_ref[...], sa_ref[...], sb_ref[...]
    cq = _rms(dq_ref[...], qn_ref[...]).astype(BF16)
    qm = _dot(cq, wuq_ref[...])
    for h in range(MLA_HEADS):
        blk = qm[:, h * HEAD_PAD:(h + 1) * HEAD_PAD]
        q_ref[:, h * HEAD_PAD:(h + 1) * HEAD_PAD] = _rope_block(blk, cos_t, sin_a, sin_b).astype(BF16)
    c_ref[...] = _rms(dkv_ref[...], kvn_ref[...])
    akr = akr_ref[...]
    y = _rope_block(akr, cos_t, sin_a, sin_b)
    lane = lax.broadcasted_iota(jnp.int32, y.shape, 1)
    rope_lane = (lane >= ROPE_LO) & (lane < ROPE_LO + MLA_ROPE)
    kpe_ref[...] = jnp.where(rope_lane, y, 0.0).astype(BF16)
    kr_ref[...] = y[:, ROPE_LO:ROPE_LO + MLA_ROPE]
    xa = _dot(akr.astype(BF16), w2_ref[...]) + b2_ref[...]
    log_sig = jnp.minimum(xa, 0.0) - jnp.log(1.0 + jnp.exp(-jnp.abs(xa)))
    loga_ref[...] = log_sig * (1.0 / GLA_TAU)


def ab_proj(z, cols, tabs, q_norm, kv_norm, wuq_pad, w2_pad, b2, *, name):
    r = z.shape[0]
    tm = _row_tile(r, 512)
    dq0, dkv0, akr0 = cols
    ql, kvl = q_norm.shape[0], kv_norm.shape[0]
    nq = wuq_pad.shape[1]
    gw = w2_pad.shape[1]
    row = lambda w: pl.BlockSpec((tm, w), lambda i: (i, 0))
    return pl.pallas_call(
        _ab_proj_kernel,
        grid=(r // tm,),
        in_specs=[
            pl.BlockSpec((tm, ql), lambda i: (i, dq0 // ql)),
            pl.BlockSpec((tm, kvl), lambda i: (i, dkv0 // kvl)),
            pl.BlockSpec((tm, LANES), lambda i: (i, akr0 // LANES)),
            row(LANES), row(LANES), row(LANES),
            _const_spec((1, ql)), _const_spec((1, kvl)),
            _const_spec((ql, nq)), _const_spec((LANES, gw)), _const_spec((1, gw)),
        ],
        out_specs=[row(gw), row(nq), row(kvl), row(MLA_ROPE), row(LANES)],
        out_shape=[
            jax.ShapeDtypeStruct((r, gw), F32),
            jax.ShapeDtypeStruct((r, nq), BF16),
            jax.ShapeDtypeStruct((r, kvl), F32),
            jax.ShapeDtypeStruct((r, MLA_ROPE), F32),
            jax.ShapeDtypeStruct((r, LANES), BF16),
        ],
        compiler_params=_cparams(("parallel",)),
        name=name,
    )(z, z, z, *tabs, q_norm.reshape(1, ql), kv_norm.reshape(1, kvl), wuq_pad, w2_pad, b2.reshape(1, gw))


def _seg_cumsum(x, rg, group):
    k = 1
    while k < group:
        x = x + jnp.where(rg >= k, pltpu.roll(x, k, 0), 0.0)
        k *= 2
    return x


def _seg_first(x, rg, group):
    y = jnp.where(rg == 0, x, 0.0)
    k = 1
    while k < group:
        y = y + pltpu.roll(y, k, 0)
        k *= 2
    return y


def _seg_last(x, rg, group):
    n = x.shape[0]
    y = jnp.where(rg == group - 1, x, 0.0)
    k = 1
    while k < group:
        y = y + pltpu.roll(y, n - k, 0)
        k *= 2
    return y


def _gla_out(o, gate, gn):
    return _rms(o, gn) * (gate * _sigmoid(gate))


def _gla_prompt_kernel(q_ref, k_ref, v_ref, gate_ref, la_ref, gn_ref, o_ref, sf_ref, s_ref, *, nchunk):
    c = GLA_CHUNK
    j = pl.program_id(2)

    @pl.when(j == 0)
    def _():
        s_ref[...] = jnp.zeros_like(s_ref)

    gn = gn_ref[...]
    row = lax.broadcasted_iota(jnp.int32, (c, LANES), 0)
    lane = lax.broadcasted_iota(jnp.int32, (c, LANES), 1)
    head_lane = [lane < GLA_DK, lane >= GLA_DK]
    rg = row & (GLA_SUB - 1)
    t_i = lax.broadcasted_iota(jnp.int32, (c, c), 0)
    s_i = lax.broadcasted_iota(jnp.int32, (c, c), 1)
    sub_shift = GLA_SUB.bit_length() - 1
    tb, sb = t_i >> sub_shift, s_i >> sub_shift
    m_diag = (tb == sb) & (s_i <= t_i)
    m_next = (tb == sb + 1) & ((sb & 1) == 0)
    m_half = (t_i >= c // 2) & (s_i < c // 2)

    def chunk(ci, carry):
        r0 = pl.multiple_of(ci * c, c)
        q = q_ref[pl.ds(r0, c), :] * (GLA_DK ** -0.5)
        k = k_ref[pl.ds(r0, c), :]
        v = v_ref[pl.ds(r0, c), :]
        gate = gate_ref[pl.ds(r0, c), :]
        b = _seg_cumsum(la_ref[pl.ds(r0, c), :], row, c)
        r_sub = _seg_first(b, rg, GLA_SUB)
        r_next = pltpu.roll(r_sub, c - GLA_SUB, 0)
        b_half = b[c // 2:c // 2 + 1, :]
        q_sub = q * jnp.exp(b - r_sub)
        k_sub = k * jnp.exp(r_sub - b)
        k_next = k * jnp.exp(jnp.minimum(r_next - b, 0.0))
        q_half = q * jnp.exp(jnp.minimum(b - b_half, 0.0))
        k_half = k * jnp.exp(jnp.minimum(b_half - b, 0.0))
        q_state = q * jnp.exp(b)
        k_t = k.T
        b_t = b.T
        b_last = b_t[:, c - 1:c]
        k_upd = k_t * jnp.exp(b_last - b_t)
        s_decay = jnp.exp(b_last)
        for h in range(2):
            hm = head_lane[h]
            z = lambda a: jnp.where(hm, a, 0.0)
            attn = (jnp.where(m_diag, _dot_nt(z(q_sub), k_sub), 0.0)
                    + jnp.where(m_next, _dot_nt(z(q_sub), k_next), 0.0)
                    + jnp.where(m_half, _dot_nt(z(q_half), k_half), 0.0))
            vh = v[:, h * GLA_DV:(h + 1) * GLA_DV]
            o = _dot(z(q_state), s_ref[...]) + _dot(attn, vh)
            o_ref[pl.ds(r0, c), h * GLA_DV:(h + 1) * GLA_DV] = _gla_out(
                o, gate[:, h * GLA_DV:(h + 1) * GLA_DV], gn).astype(BF16)
        for h in range(2):
            lo, hi = h * GLA_DK, (h + 1) * GLA_DK
            vh = v[:, h * GLA_DV:(h + 1) * GLA_DV]
            s_ref[lo:hi, :] = s_decay[lo:hi] * s_ref[lo:hi, :] + _dot(k_upd[lo:hi], vh)
        return carry

    lax.fori_loop(0, nchunk, chunk, 0)

    @pl.when(j == pl.num_programs(2) - 1)
    def _():
        sf_ref[0, 0] = s_ref[0:GLA_DK, :]
        sf_ref[0, 1] = s_ref[GLA_DK:2 * GLA_DK, :]


def gla_prompt(z, loga, gla_norm, cols, bp, s, *, name):
    q0, k0, v0, g0 = cols
    tb = _row_tile(s, 512)
    assert tb % GLA_CHUNK == 0
    nj = s // tb
    npair = GLA_HEADS // 2
    qk_w, v_w = 2 * GLA_DK, 2 * GLA_DV
    rowblk = lambda b, p, j: b * nj + j
    return pl.pallas_call(
        functools.partial(_gla_prompt_kernel, nchunk=tb // GLA_CHUNK),
        grid=(bp, npair, nj),
        in_specs=[
            pl.BlockSpec((tb, qk_w), lambda b, p, j: (rowblk(b, p, j), q0 // qk_w + p)),
            pl.BlockSpec((tb, qk_w), lambda b, p, j: (rowblk(b, p, j), k0 // qk_w + p)),
            pl.BlockSpec((tb, v_w), lambda b, p, j: (rowblk(b, p, j), v0 // v_w + p)),
            pl.BlockSpec((tb, v_w), lambda b, p, j: (rowblk(b, p, j), g0 // v_w + p)),
            pl.BlockSpec((tb, qk_w), lambda b, p, j: (rowblk(b, p, j), p)),
            pl.BlockSpec((1, GLA_DV), lambda b, p, j: (0, 0)),
        ],
        out_specs=[
            pl.BlockSpec((tb, v_w), lambda b, p, j: (rowblk(b, p, j), p)),
            pl.BlockSpec((1, 2, GLA_DK, GLA_DV), lambda b, p, j: (b, p, 0, 0)),
        ],
        out_shape=[
            jax.ShapeDtypeStruct((bp * s, GLA_HEADS * GLA_DV), BF16),
            jax.ShapeDtypeStruct((bp, GLA_HEADS, GLA_DK, GLA_DV), F32),
        ],
        scratch_shapes=[pltpu.VMEM((2 * GLA_DK, GLA_DV), F32)],
        compiler_params=_cparams(("parallel", "parallel", "arbitrary")),
        name=name,
    )(z, z, z, z, loga, gla_norm.reshape(1, GLA_DV))


def _gla_sample_kernel(q_ref, k_ref, v_ref, gate_ref, la_ref, gn_ref, s0_ref, o_ref, sn_ref, *, t, nseq):
    n = nseq * t
    gn = gn_ref[...]
    row = lax.broadcasted_iota(jnp.int32, (n, LANES), 0)
    lane = lax.broadcasted_iota(jnp.int32, (n, LANES), 1)
    head_lane = [lane < GLA_DK, lane >= GLA_DK]
    t_shift, lane_shift = t.bit_length() - 1, LANES.bit_length() - 1
    rg = row & (t - 1)
    t_i = lax.broadcasted_iota(jnp.int32, (n, n), 0)
    s_i = lax.broadcasted_iota(jnp.int32, (n, n), 1)
    m_seq = ((t_i >> t_shift) == (s_i >> t_shift)) & (s_i <= t_i)
    wide = nseq * LANES
    w_row = lax.broadcasted_iota(jnp.int32, (n, wide), 0)
    w_col = lax.broadcasted_iota(jnp.int32, (n, wide), 1)
    own_seq = (w_row >> t_shift) == (w_col >> lane_shift)
    own_last = own_seq & ((w_row & (t - 1)) == t - 1)
    w_head = [(w_col & GLA_DK) == 0, (w_col & GLA_DK) != 0]
    srow_head1 = (lax.broadcasted_iota(jnp.int32, (wide, GLA_DV), 0) & GLA_DK) != 0

    q = q_ref[...] * (GLA_DK ** -0.5)
    k = k_ref[...]
    v = v_ref[...]
    gate = gate_ref[...]
    b = _seg_cumsum(la_ref[...], rg, t)
    r_first = _seg_first(b, rg, t)
    r_last = _seg_last(b, rg, t)
    q_sub = q * jnp.exp(b - r_first)
    k_sub = k * jnp.exp(r_first - b)
    q_state = q * jnp.exp(b)
    k_upd = k * jnp.exp(r_last - b)
    s_all = s0_ref[...].reshape(wide, GLA_DV)
    tile = lambda a: jnp.concatenate([a] * nseq, axis=1)
    q_wide = jnp.where(own_seq, tile(q_state), 0.0)
    k_wide_t = jnp.where(own_seq, tile(k_upd), 0.0).T
    decay = jnp.exp(jnp.sum(jnp.where(own_last, tile(b), 0.0).T, axis=1, keepdims=True))
    upd = []
    for h in range(2):
        attn = jnp.where(m_seq, _dot_nt(jnp.where(head_lane[h], q_sub, 0.0), k_sub), 0.0)
        vh = v[:, h * GLA_DV:(h + 1) * GLA_DV]
        o = _dot(jnp.where(w_head[h], q_wide, 0.0), s_all) + _dot(attn, vh)
        o_ref[:, h * GLA_DV:(h + 1) * GLA_DV] = _gla_out(o, gate[:, h * GLA_DV:(h + 1) * GLA_DV], gn).astype(BF16)
        upd.append(_dot(k_wide_t, vh))
    s_new = decay * s_all + jnp.where(srow_head1, upd[1], upd[0])
    sn_ref[...] = s_new.reshape(sn_ref.shape)


def gla_sample(z, loga, gla_norm, cols, s0, row0, bd, t, *, name):
    q0, k0, v0, g0 = cols
    nseq = 16 if bd % 16 == 0 else bd
    n = nseq * t
    assert n % 8 == 0 and row0 % n == 0
    npair = GLA_HEADS // 2
    qk_w, v_w = 2 * GLA_DK, 2 * GLA_DV
    off = row0 // n
    return pl.pallas_call(
        functools.partial(_gla_sample_kernel, t=t, nseq=nseq),
        grid=(bd // nseq, npair),
        in_specs=[
            pl.BlockSpec((n, qk_w), lambda i, p: (i + off, q0 // qk_w + p)),
            pl.BlockSpec((n, qk_w), lambda i, p: (i + off, k0 // qk_w + p)),
            pl.BlockSpec((n, v_w), lambda i, p: (i + off, v0 // v_w + p)),
            pl.BlockSpec((n, v_w), lambda i, p: (i + off, g0 // v_w + p)),
            pl.BlockSpec((n, qk_w), lambda i, p: (i + off, p)),
            pl.BlockSpec((1, GLA_DV), lambda i, p: (0, 0)),
            pl.BlockSpec((nseq, 2, GLA_DK, GLA_DV), lambda i, p: (i, p, 0, 0)),
        ],
        out_specs=[
            pl.BlockSpec((n, v_w), lambda i, p: (i, p)),
            pl.BlockSpec((nseq, 2, GLA_DK, GLA_DV), lambda i, p: (i, p, 0, 0)),
        ],
        out_shape=[
            jax.ShapeDtypeStruct((bd * t, GLA_HEADS * GLA_DV), BF16),
            jax.ShapeDtypeStruct((bd, GLA_HEADS, GLA_DK, GLA_DV), F32),
        ],
        compiler_params=_cparams(("parallel", "parallel")),
        name=name,
    )(z, z, z, z, loga, gla_norm.reshape(1, GLA_DV), s0)


def _mla_self_kernel(q_ref, c_ref, kpe_ref, w_ref, o_ref, k_scr, v_scr, *, tq):
    i = pl.program_id(2)

    @pl.when(i == 0)
    def _():
        kv = _dot(c_ref[...].astype(BF16), w_ref[0])
        kpe = kpe_ref[...].astype(F32)
        for h in range(2):
            k_scr[h] = (kv[:, h * HEAD_PAD:(h + 1) * HEAD_PAD] + kpe).astype(BF16)
            v_scr[h] = kv[:, (2 + h) * HEAD_PAD:(3 + h) * HEAD_PAD].astype(BF16)

    t_i = lax.broadcasted_iota(jnp.int32, (tq, tq), 0)
    s_i = lax.broadcasted_iota(jnp.int32, (tq, tq), 1)
    causal = s_i <= t_i
    out = jnp.zeros((tq, HEAD_PAD), F32)
    for h in range(2):
        q = q_ref[:, h * HEAD_PAD:(h + 1) * HEAD_PAD]

        def step(kb, carry, masked):
            m, l, acc = carry
            r0 = pl.multiple_of(kb * tq, tq)
            s = _dot_nt(q, k_scr[h, pl.ds(r0, tq), :]) * MLA_SCALE
            if masked:
                s = jnp.where(causal, s, -jnp.inf)
            m_new = jnp.maximum(m, jnp.max(s, axis=1, keepdims=True))
            alpha = jnp.exp(m - m_new)
            p = jnp.exp(s - m_new)
            l = alpha * l + jnp.sum(p, axis=1, keepdims=True)
            acc = alpha * acc + _dot(p.astype(BF16), v_scr[h, pl.ds(r0, tq), :])
            return m_new, l, acc

        init = (jnp.full((tq, 1), -jnp.inf, F32), jnp.zeros((tq, 1), F32), jnp.zeros((tq, HEAD_PAD), F32))
        carry = lax.fori_loop(0, i, functools.partial(step, masked=False), init)
        _, l, acc = step(i, carry, True)
        out = out + acc / l
    o_ref[...] = out.astype(BF16)


def mla_self(qp, c_new, kpe, wkv_pair, bp, s, *, name):
    tq = _row_tile(s, 256)
    nq = s // tq
    npair = MLA_HEADS // 2
    kvl = c_new.shape[1]
    return pl.pallas_call(
        functools.partial(_mla_self_kernel, tq=tq),
        grid=(bp, npair, nq),
        in_specs=[
            pl.BlockSpec((tq, 2 * HEAD_PAD), lambda b, p, i: (b * nq + i, p)),
            pl.BlockSpec((s, kvl), lambda b, p, i: (b, 0)),
            pl.BlockSpec((s, LANES), lambda b, p, i: (b, 0)),
            pl.BlockSpec((1, kvl, 4 * HEAD_PAD), lambda b, p, i: (p, 0, 0)),
        ],
        out_specs=pl.BlockSpec((tq, 2 * MLA_V), lambda b, p, i: (b * nq + i, p)),
        out_shape=jax.ShapeDtypeStruct((bp * s, MLA_HEADS * MLA_V), BF16),
        scratch_shapes=[pltpu.VMEM((2, s, HEAD_PAD), BF16), pltpu.VMEM((2, s, HEAD_PAD), BF16)],
        compiler_params=_cparams(("parallel", "parallel", "arbitrary")),
        name=name,
    )(qp, c_new, kpe, wkv_pair)


def _q_absorb_kernel(q_ref, w_ref, o_ref):
    o_ref[0] = _dot(q_ref[...], w_ref[0])


def q_absorb(qp, wq_abs, row0, rows, *, name):
    assert row0 % rows == 0
    off = row0 // rows
    wout = wq_abs.shape[2]
    return pl.pallas_call(
        _q_absorb_kernel,
        grid=(MLA_HEADS,),
        in_specs=[
            pl.BlockSpec((rows, HEAD_PAD), lambda h: (off, h)),
            pl.BlockSpec((1, HEAD_PAD, wout), lambda h: (h, 0, 0)),
        ],
        out_specs=pl.BlockSpec((1, rows, wout), lambda h: (h, 0, 0)),
        out_shape=jax.ShapeDtypeStruct((MLA_HEADS, rows, wout), F32),
        compiler_params=_cparams(("parallel",)),
        name=name,
    )(qp, wq_abs)


def _mla_cached_kernel(pt_ref, q_ref, cn_ref, krn_ref, ckv_hbm, ckr_hbm, o_ref, kv_buf, kr_buf, sem,
                       *, layer, npages, page, t, kvl):
    b = pl.program_id(0)
    nb = pl.num_programs(0)
    slot = b % 2

    def copies(seq, sl, p):
        pg = pt_ref[seq, p]
        rows = pl.ds(pl.multiple_of(p * page, page), page)
        return (pltpu.make_async_copy(ckv_hbm.at[layer, pg], kv_buf.at[sl, rows], sem.at[0, sl]),
                pltpu.make_async_copy(ckr_hbm.at[layer, pg], kr_buf.at[sl, rows], sem.at[1, sl]))

    def start_all(seq, sl):
        def body(p, carry):
            for cp in copies(seq, sl, p):
                cp.start()
            return carry
        lax.fori_loop(0, npages, body, 0)

    @pl.when(b == 0)
    def _():
        start_all(0, 0)

    @pl.when(b + 1 < nb)
    def _():
        start_all(b + 1, 1 - slot)

    def wait_body(p, carry):
        for cp in copies(b, slot, p):
            cp.wait()
        return carry
    lax.fori_loop(0, npages, wait_body, 0)

    q = q_ref[0]
    q_lat = q[:, :kvl]
    q_pe = q[:, kvl:kvl + MLA_ROPE]
    past_c = kv_buf[slot]
    past_r = kr_buf[slot]
    s_past = (_dot_nt(q_lat, past_c) + _dot_nt(q_pe, past_r)) * MLA_SCALE
    c_new = cn_ref[0]
    kr_new = krn_ref[0]
    n = q.shape[0]
    tok = lax.broadcasted_iota(jnp.int32, (n, 1), 0) & (t - 1)
    s_new = []
    for jn in range(t):
        sj = (jnp.sum(q_lat * c_new[jn:jn + 1, :], axis=1, keepdims=True)
              + jnp.sum(q_pe * kr_new[jn:jn + 1, :], axis=1, keepdims=True)) * MLA_SCALE
        s_new.append(jnp.where(tok >= jn, sj, -jnp.inf))
    m = jnp.max(s_past, axis=1, keepdims=True)
    for sj in s_new:
        m = jnp.maximum(m, sj)
    p_past = jnp.exp(s_past - m)
    l = jnp.sum(p_past, axis=1, keepdims=True)
    o = _dot(p_past, past_c)
    for jn, sj in enumerate(s_new):
        pj = jnp.exp(sj - m)
        l = l + pj
        o = o + pj * c_new[jn:jn + 1, :]
    o_ref[0] = o / l


def mla_cached(page_table, q_abs, c_new_s, kr_new_s, cache_kv, cache_kr, layer, *, name):
    bd, npages = page_table.shape
    page, kvl = cache_kv.shape[2], cache_kv.shape[3]
    rope = cache_kr.shape[3]
    t = c_new_s.shape[1]
    n, qw = q_abs.shape[1], q_abs.shape[2]
    past = npages * page
    grid_spec = pltpu.PrefetchScalarGridSpec(
        num_scalar_prefetch=1,
        grid=(bd,),
        in_specs=[
            pl.BlockSpec((1, n, qw), lambda b, pt: (b, 0, 0)),
            pl.BlockSpec((1, t, kvl), lambda b, pt: (b, 0, 0)),
            pl.BlockSpec((1, t, rope), lambda b, pt: (b, 0, 0)),
            pl.BlockSpec(memory_space=pl.ANY),
            pl.BlockSpec(memory_space=pl.ANY),
        ],
        out_specs=pl.BlockSpec((1, n, kvl), lambda b, pt: (b, 0, 0)),
        scratch_shapes=[
            pltpu.VMEM((2, past, kvl), F32),
            pltpu.VMEM((2, past, rope), F32),
            pltpu.SemaphoreType.DMA((2, 2)),
        ],
    )
    return pl.pallas_call(
        functools.partial(_mla_cached_kernel, layer=layer, npages=npages, page=page, t=t, kvl=kvl),
        grid_spec=grid_spec,
        out_shape=jax.ShapeDtypeStruct((bd, n, kvl), F32),
        compiler_params=_cparams(("arbitrary",)),
        name=name,
    )(page_table, q_abs, c_new_s, kr_new_s, cache_kv, cache_kr)


def _v_up_kernel(o_ref, w_ref, out_ref):
    out_ref[...] = (_dot(o_ref[0].astype(BF16), w_ref[0, 0]) + _dot(o_ref[1].astype(BF16), w_ref[0, 1])).astype(BF16)


def v_up(o_lat, wv_pair, *, name):
    h, rows, kvl = o_lat.shape
    return pl.pallas_call(
        _v_up_kernel,
        grid=(h // 2,),
        in_specs=[
            pl.BlockSpec((2, rows, kvl), lambda p: (p, 0, 0)),
            pl.BlockSpec((1, 2, kvl, 2 * MLA_V), lambda p: (p, 0, 0, 0)),
        ],
        out_specs=pl.BlockSpec((rows, 2 * MLA_V), lambda p: (0, p)),
        out_shape=jax.ShapeDtypeStruct((rows, h * MLA_V), BF16),
        compiler_params=_cparams(("parallel",)),
        name=name,
    )(o_lat, wv_pair)


CONV_PAD = 32


def _conv_kernel(*refs, tt, bb, cw, has_halo):
    if has_halo:
        u_ref, halo_ref, buf_ref, x_ref, wdw_ref, bdw_ref, lg_ref, lb_ref, w2_ref, b2_ref, o_ref, st_ref, ext_ref, h_ref = refs
    else:
        u_ref, buf_ref, x_ref, wdw_ref, bdw_ref, lg_ref, lb_ref, w2_ref, b2_ref, o_ref, st_ref, ext_ref, h_ref = refs
    hist = cw - 1
    lo = CONV_PAD - hist
    d = u_ref.shape[2]
    j = pl.program_id(1)
    sub = min(tt, 64)
    lane_chunk = 256
    for i in range(bb):
        if has_halo:
            @pl.when(j == 0)
            def _():
                ext_ref[i, lo:CONV_PAD, :] = buf_ref[i]

            @pl.when(j > 0)
            def _():
                ext_ref[i, 0:CONV_PAD, :] = halo_ref[i]
        else:
            ext_ref[i, lo:CONV_PAD, :] = buf_ref[i]
        ext_ref[i, CONV_PAD:CONV_PAD + tt, :] = u_ref[i]
        st_ref[i] = ext_ref[i, CONV_PAD + tt - hist:CONV_PAD + tt, :]
        for r0 in range(0, tt, sub):
            for c0 in range(0, d, lane_chunk):
                cs = slice(c0, c0 + lane_chunk)
                acc = jnp.broadcast_to(bdw_ref[:, cs], (sub, lane_chunk))
                for w in range(cw):
                    acc = acc + ext_ref[i, lo + r0 + w:lo + r0 + w + sub, cs] * wdw_ref[w:w + 1, cs]
                h_ref[i * tt + r0:i * tt + r0 + sub, cs] = acc
            y = h_ref[i * tt + r0:i * tt + r0 + sub, :]
            mu = jnp.mean(y, axis=-1, keepdims=True)
            yc = y - mu
            var = jnp.mean(yc * yc, axis=-1, keepdims=True)
            yn = yc * lax.rsqrt(var + EPS) * lg_ref[...] + lb_ref[...]
            h_ref[i * tt + r0:i * tt + r0 + sub, :] = yn * _sigmoid(yn)
    out = _dot(h_ref[...].astype(BF16), w2_ref[...]) + b2_ref[...]
    for i in range(bb):
        o_ref[i] = x_ref[i] + out[i * tt:(i + 1) * tt, :]


def conv_module(u, buf, x, w_dw, b_dw, ln_g, ln_b, w2_bf16, b2, *, name):
    bsz, t, d = u.shape
    cw = w_dw.shape[0]
    hist = cw - 1
    assert hist <= CONV_PAD
    has_halo = t > 512
    if has_halo:
        tt, bb = 512, 1
        assert t % tt == 0
    else:
        tt = t
        bb = 32 if bsz % 32 == 0 else bsz
    nj = t // tt
    blk = lambda rows: pl.BlockSpec((bb, rows, d), lambda b, j: (b, j, 0))
    in_specs = [blk(tt)]
    args = [u]
    if has_halo:
        hb = tt // CONV_PAD
        in_specs.append(pl.BlockSpec((bb, CONV_PAD, d), lambda b, j: (b, jnp.maximum(j * hb - 1, 0), 0)))
        args.append(u)
    in_specs += [
        pl.BlockSpec((bb, hist, d), lambda b, j: (b, 0, 0)),
        blk(tt),
        _const_spec((cw, d)), _const_spec((1, d)), _const_spec((1, d)), _const_spec((1, d)),
        _const_spec((d, d)), _const_spec((1, d)),
    ]
    args += [buf, x, w_dw, b_dw.reshape(1, d), ln_g.reshape(1, d), ln_b.reshape(1, d), w2_bf16, b2.reshape(1, d)]
    ext_rows = -(-(CONV_PAD + tt) // 8) * 8
    return pl.pallas_call(
        functools.partial(_conv_kernel, tt=tt, bb=bb, cw=cw, has_halo=has_halo),
        grid=(bsz // bb, nj),
        in_specs=in_specs,
        out_specs=[blk(tt), pl.BlockSpec((bb, hist, d), lambda b, j: (b, 0, 0))],
        out_shape=[jax.ShapeDtypeStruct((bsz, t, d), F32), jax.ShapeDtypeStruct((bsz, hist, d), F32)],
        scratch_shapes=[pltpu.VMEM((bb, ext_rows, d), F32), pltpu.VMEM((bb * tt, d), F32)],
        compiler_params=_cparams(("parallel", "arbitrary")),
        name=name,
    )(*args)


def _rope_tables(pos):
    inv = jnp.power(ROPE_THETA, -jnp.arange(0, MLA_ROPE, 2, dtype=F32) / MLA_ROPE)
    ang = pos.astype(F32)[:, None] * inv[None, :]
    cos, sin = jnp.cos(ang), jnp.sin(ang)
    n = pos.shape[0]
    lo = ROPE_LO
    cos_t = jnp.ones((n, LANES), F32).at[:, lo:lo + MLA_ROPE].set(jnp.concatenate([cos, cos], axis=1))
    sin_a = jnp.zeros((n, LANES), F32).at[:, lo:lo + ROPE_HALF].set(-sin)
    sin_b = jnp.zeros((n, LANES), F32).at[:, lo + ROPE_HALF:lo + MLA_ROPE].set(sin)
    return cos_t, sin_a, sin_b


def _layout_w_in(w_in, gla_qk, gla_v, q_lora, kv_lora):
    d = w_in.shape[0]
    sp = [0, gla_qk, 2 * gla_qk, 2 * gla_qk + gla_v, 2 * gla_qk + gla_v + GLA_RANK]
    sp += [sp[-1] + gla_v, sp[-1] + gla_v + q_lora, sp[-1] + gla_v + q_lora + kv_lora]
    q, k, v, a, g, dq, dkv, kr = [w_in[:, lo:hi] for lo, hi in zip(sp, sp[1:] + [w_in.shape[1]])]
    last = jnp.zeros((d, LANES), w_in.dtype).at[:, :GLA_RANK].set(a).at[:, ROPE_LO:ROPE_LO + MLA_ROPE].set(kr)
    w = jnp.concatenate([q, k, v, g, dq, dkv, last], axis=1)
    cols = dict(q=0, k=gla_qk, v=2 * gla_qk, g=2 * gla_qk + gla_v, dq=2 * gla_qk + 2 * gla_v)
    cols["dkv"] = cols["dq"] + q_lora
    cols["akr"] = cols["dkv"] + kv_lora
    return w.astype(BF16), cols


def _layout_w_uq(w_uq):
    ql = w_uq.shape[0]
    w = w_uq.reshape(ql, MLA_HEADS, MLA_NOPE + MLA_ROPE)
    w = jnp.pad(w, ((0, 0), (0, 0), (0, HEAD_PAD - MLA_NOPE - MLA_ROPE)))
    return w.reshape(ql, MLA_HEADS * HEAD_PAD).astype(BF16)


def _layout_w_ukv(w_ukv):
    kvl = w_ukv.shape[0]
    w = w_ukv.reshape(kvl, MLA_HEADS, MLA_NOPE + MLA_V)
    wk = jnp.pad(w[..., :MLA_NOPE], ((0, 0), (0, 0), (0, HEAD_PAD - MLA_NOPE)))
    wv = w[..., MLA_NOPE:]
    wv_e = jnp.pad(wv[:, 0::2], ((0, 0), (0, 0), (0, MLA_V)))
    wv_o = jnp.pad(wv[:, 1::2], ((0, 0), (0, 0), (MLA_V, 0)))
    pair = jnp.stack([wk[:, 0::2], wk[:, 1::2], wv_e, wv_o], axis=2)
    pair = pair.transpose(1, 0, 2, 3).reshape(MLA_HEADS // 2, kvl, 4 * HEAD_PAD)
    w_uk_t = w[..., :MLA_NOPE].transpose(1, 2, 0)
    wq_abs = jnp.zeros((MLA_HEADS, HEAD_PAD, kvl + LANES), F32)
    wq_abs = wq_abs.at[:, :MLA_NOPE, :kvl].set(w_uk_t)
    wq_abs = wq_abs.at[:, ROPE_LO:ROPE_LO + MLA_ROPE, kvl:kvl + MLA_ROPE].set(jnp.eye(MLA_ROPE, dtype=F32))
    wv_h = wv.transpose(1, 0, 2)
    wv_pair = jnp.stack([jnp.pad(wv_h[0::2], ((0, 0), (0, 0), (0, MLA_V))),
                         jnp.pad(wv_h[1::2], ((0, 0), (0, 0), (MLA_V, 0)))], axis=1)
    return pair.astype(BF16), wq_abs.astype(BF16), wv_pair.astype(BF16)


def kernel(x_prompt, x_sample, cache_kv, cache_kr, state_gla, state_conv, page_table, norm_mix, norm_mlp,
           norm_final, w_in, w_gate_a2, b_gate_a, gla_norm, mla_q_norm, mla_kv_norm, w_uq, w_ukv, w_out_ab,
           w_pw1, b_pw1, w_dw, b_dw, conv_ln_g, conv_ln_b, w_pw2, b_pw2, w_up, w_down):
    bp, s, d = x_prompt.shape
    bd, t, _ = x_sample.shape
    rp, rs = bp * s, bd * t
    depth = norm_mix.shape[0]
    page = cache_kv.shape[2]
    past = page_table.shape[1] * page
    gla_qk, gla_v = GLA_HEADS * GLA_DK, GLA_HEADS * GLA_DV
    q_lora, kv_lora = mla_q_norm.shape[1], mla_kv_norm.shape[1]
    d_ff = w_up.shape[2]
    tf = 512 if d_ff % 512 == 0 else d_ff
    cw = w_dw.shape[1]

    pos = jnp.concatenate([jnp.tile(jnp.arange(s), bp), jnp.tile(past + jnp.arange(t), bd)])
    tabs = _rope_tables(pos)

    x = jnp.concatenate([x_prompt.reshape(rp, d), x_sample.reshape(rs, d)], axis=0)
    kv_p, kr_p, gla_p, conv_p, kv_s, kr_s, gla_s, conv_s = ([] for _ in range(8))
    for l in range(depth):
        i = l // 2
        if l % 2 == 0:
            w_in_l, cols = _layout_w_in(w_in[i], gla_qk, gla_v, q_lora, kv_lora)
            z = norm_matmul(x, norm_mix[l], w_in_l, jnp.zeros((w_in_l.shape[1],), F32), name=f"in_proj{l}")
            w2_pad = jnp.zeros((LANES, gla_qk), F32).at[:GLA_RANK].set(w_gate_a2[i]).astype(BF16)
            loga, qp, c_new, kr_new, kpe = ab_proj(
                z, (cols["dq"], cols["dkv"], cols["akr"]), tabs, mla_q_norm[i], mla_kv_norm[i],
                _layout_w_uq(w_uq[i]), w2_pad, b_gate_a[i], name=f"ab_proj{l}")
            gcols = (cols["q"], cols["k"], cols["v"], cols["g"])
            og_p, sg_p = gla_prompt(z, loga, gla_norm[i], gcols, bp, s, name=f"gla_prompt{l}")
            og_s, sg_s = gla_sample(z, loga, gla_norm[i], gcols, state_gla[i], rp, bd, t, name=f"gla_sample{l}")
            wkv_pair, wq_abs, wv_pair = _layout_w_ukv(w_ukv[i])
            om_p = mla_self(qp, c_new, kpe, wkv_pair, bp, s, name=f"mla_self{l}")
            q_abs = q_absorb(qp, wq_abs, rp, rs, name=f"q_absorb{l}")
            q_abs = q_abs.reshape(MLA_HEADS, bd, t, -1).transpose(1, 0, 2, 3).reshape(bd, MLA_HEADS * t, -1)
            c_new_s = c_new[rp:].reshape(bd, t, kv_lora)
            kr_new_s = kr_new[rp:].reshape(bd, t, MLA_ROPE)
            o_lat = mla_cached(page_table, q_abs, c_new_s, kr_new_s, cache_kv, cache_kr, i, name=f"mla_cached{l}")
            o_lat = o_lat.reshape(bd, MLA_HEADS, t, kv_lora).transpose(1, 0, 2, 3).reshape(MLA_HEADS, rs, kv_lora)
            om_s = v_up(o_lat, wv_pair, name=f"v_up{l}")
            a = jnp.concatenate([jnp.concatenate([og_p, om_p], axis=1),
                                 jnp.concatenate([og_s, om_s], axis=1)], axis=0)
            x = out_proj_residual(a, w_out_ab[i].astype(BF16), x, name=f"out_proj{l}")
            kv_p.append(c_new[:rp].reshape(bp, s, kv_lora)); kr_p.append(kr_new[:rp].reshape(bp, s, MLA_ROPE))
            kv_s.append(c_new_s); kr_s.append(kr_new_s)
            gla_p.append(sg_p); gla_s.append(sg_s)
        else:
            u = norm_matmul(x, norm_mix[l], w_pw1[i].astype(BF16), b_pw1[i], glu=True, name=f"pw1_glu{l}")
            cargs = (w_dw[i], b_dw[i], conv_ln_g[i], conv_ln_b[i], w_pw2[i].astype(BF16), b_pw2[i])
            xp, st_p = conv_module(u[:rp].reshape(bp, s, d), jnp.zeros((bp, cw - 1, d), F32),
                                   x[:rp].reshape(bp, s, d), *cargs, name=f"conv_prompt{l}")
            xs, st_s = conv_module(u[rp:].reshape(bd, t, d), state_conv[i],
                                   x[rp:].reshape(bd, t, d), *cargs, name=f"conv_sample{l}")
            x = jnp.concatenate([xp.reshape(rp, d), xs.reshape(rs, d)], axis=0)
            conv_p.append(st_p); conv_s.append(st_s)
        wup3 = w_up[l].astype(BF16).reshape(d, d_ff // tf, tf).transpose(1, 0, 2)
        wdn3 = w_down[l].astype(BF16).reshape(d_ff // tf, tf, d)
        x = mlp_residual(x, norm_mlp[l], wup3, wdn3, name=f"mlp{l}")
    y_p = final_norm(x, norm_final, 0, rp, name="final_norm_prompt").reshape(bp, s, d)
    y_s = final_norm(x, norm_final, rp, rs, name="final_norm_sample").reshape(bd, t, d)
    return (y_p, y_s, jnp.stack(kv_p), jnp.stack(kr_p), jnp.stack(gla_p), jnp.stack(conv_p),
            jnp.stack(kv_s), jnp.stack(kr_s), jnp.stack(gla_s), jnp.stack(conv_s))
```

```python
import functools
import math

import jax
import jax.numpy as jnp
from jax import lax
from jax.experimental import pallas as pl
from jax.experimental.pallas import tpu as pltpu

F32 = jnp.float32
BF16 = jnp.bfloat16

GLA_HEADS = 4
GLA_DK = 64
GLA_DV = 128
GLA_RANK = 16
GLA_TAU = 16.0
GLA_CHUNK = 64
GLA_SUB = 16
MLA_HEADS = 8
MLA_NOPE = 64
MLA_ROPE = 32
MLA_V = 64
MLA_SCALE = (MLA_NOPE + MLA_ROPE) ** -0.5
Q_PRESCALE = MLA_SCALE * math.log2(math.e)
ROPE_THETA = 10000.0
EPS = 1e-6

LANES = 128
SUBLANES = 8
HEAD_PAD = 128
ROPE_LO = MLA_NOPE
ROPE_HALF = MLA_ROPE // 2
VMEM_LIMIT = 56 * 1024 * 1024


def _cparams(sem):
    return pltpu.CompilerParams(dimension_semantics=sem, vmem_limit_bytes=VMEM_LIMIT)


def _row_tile(n, target):
    best = None
    for t in range(8, min(n, target) + 1, 8):
        if n % t == 0:
            best = t
    assert best is not None, n
    return best


def _const_spec(shape):
    nd = len(shape)
    return pl.BlockSpec(shape, lambda *_: (0,) * nd, pipeline_mode=pl.Buffered(1))


_ANY = pl.BlockSpec(memory_space=pl.ANY)


def _rms(x, g):
    ms = jnp.mean(x * x, axis=-1, keepdims=True)
    return x * lax.rsqrt(ms + EPS) * g


def _sigmoid(x):
    return 1.0 / (1.0 + jnp.exp(-x))


def _dot(a, b):
    return jnp.dot(a, b, preferred_element_type=F32)


def _dot_nt(a, b):
    return lax.dot_general(a, b, (((1,), (1,)), ((), ())), preferred_element_type=F32)


def _norm_matmul_kernel(x_ref, g_ref, w_ref, b_ref, o_ref, *, glu):
    h = _rms(x_ref[...], g_ref[...]).astype(BF16)
    u = _dot(h, w_ref[...]) + b_ref[...]
    if glu:
        half = u.shape[1] // 2
        u = u[:, :half] * _sigmoid(u[:, half:])
    o_ref[...] = u


def norm_matmul(x, g, w_bf16, bias, *, glu=False, name):
    r, d = x.shape
    n = w_bf16.shape[1]
    n_out = n // 2 if glu else n
    tm = _row_tile(r, 512)
    return pl.pallas_call(
        functools.partial(_norm_matmul_kernel, glu=glu),
        grid=(r // tm,),
        in_specs=[
            pl.BlockSpec((tm, d), lambda i: (i, 0)),
            _const_spec((1, d)),
            _const_spec((d, n)),
            _const_spec((1, n)),
        ],
        out_specs=pl.BlockSpec((tm, n_out), lambda i: (i, 0)),
        out_shape=jax.ShapeDtypeStruct((r, n_out), F32),
        compiler_params=_cparams(("parallel",)),
        name=name,
    )(x, g.reshape(1, d), w_bf16, bias.reshape(1, n))


def _mlp_kernel(x_ref, g_ref, wup_ref, wdn_ref, o_ref, h_ref):
    x = x_ref[...]
    h_ref[...] = _rms(x, g_ref[...]).astype(BF16)
    o_ref[...] = x

    def body(f, carry):
        u = jnp.maximum(_dot(h_ref[...], wup_ref[f]), 0.0)
        o_ref[...] += _dot((u * u).astype(BF16), wdn_ref[f])
        return carry

    lax.fori_loop(0, wup_ref.shape[0], body, 0)


def mlp_residual(x, g, wup3, wdn3, *, name):
    r, d = x.shape
    nf, _, tf = wup3.shape
    tm = _row_tile(r, 512)
    return pl.pallas_call(
        _mlp_kernel,
        grid=(r // tm,),
        in_specs=[
            pl.BlockSpec((tm, d), lambda i: (i, 0)),
            _const_spec((1, d)),
            _const_spec((nf, d, tf)),
            _const_spec((nf, tf, d)),
        ],
        out_specs=pl.BlockSpec((tm, d), lambda i: (i, 0)),
        out_shape=jax.ShapeDtypeStruct((r, d), F32),
        scratch_shapes=[pltpu.VMEM((tm, d), BF16)],
        compiler_params=_cparams(("parallel",)),
        name=name,
    )(x, g.reshape(1, d), wup3, wdn3)


def _out_proj_kernel(a_ref, w_ref, r_ref, o_ref):
    o_ref[...] = r_ref[...] + _dot(a_ref[...], w_ref[...])


def out_proj_residual(a_bf16, w_bf16, res, *, name):
    r, k = a_bf16.shape
    d = w_bf16.shape[1]
    tm = _row_tile(r, 512)
    return pl.pallas_call(
        _out_proj_kernel,
        grid=(r // tm,),
        in_specs=[
            pl.BlockSpec((tm, k), lambda i: (i, 0)),
            _const_spec((k, d)),
            pl.BlockSpec((tm, d), lambda i: (i, 0)),
        ],
        out_specs=pl.BlockSpec((tm, d), lambda i: (i, 0)),
        out_shape=jax.ShapeDtypeStruct((r, d), F32),
        compiler_params=_cparams(("parallel",)),
        name=name,
    )(a_bf16, w_bf16, res)


def _final_norm_kernel(x_ref, g_ref, o_ref):
    o_ref[...] = _rms(x_ref[...], g_ref[...])


def final_norm(x, g, row0, rows, *, name):
    d = x.shape[1]
    tm = _row_tile(math.gcd(row0, rows) if row0 else rows, 512)
    off = row0 // tm
    return pl.pallas_call(
        _final_norm_kernel,
        grid=(rows // tm,),
        in_specs=[pl.BlockSpec((tm, d), lambda i: (i + off, 0)), _const_spec((1, d))],
        out_specs=pl.BlockSpec((tm, d), lambda i: (i, 0)),
        out_shape=jax.ShapeDtypeStruct((rows, d), F32),
        compiler_params=_cparams(("parallel",)),
        name=name,
    )(x, g.reshape(1, d))


def _rope_block(x, cos_t, sin_a, sin_b):
    up = pltpu.roll(x, LANES - ROPE_HALF, 1)
    dn = pltpu.roll(x, ROPE_HALF, 1)
    return x * cos_t + up * sin_a + dn * sin_b


def _ab_proj_kernel(dq_ref, dkv_ref, akr_ref, cos_ref, sa_ref, sb_ref, qn_ref, kvn_ref,
                    wuq_ref, w2_ref, b2_ref, loga_ref, q_ref, c_ref, kr_ref, kpe_ref):
    cos_t, sin_a, sin_b = cos_ref[...], sa_ref[...], sb_ref[...]
    cq = _rms(dq_ref[...], qn_ref[...]).astype(BF16)
    qm = _dot(cq, wuq_ref[...])
    for h in range(MLA_HEADS):
        blk = qm[:, h * HEAD_PAD:(h + 1) * HEAD_PAD]
        roped = _rope_block(blk, cos_t, sin_a, sin_b)
        q_ref[:, h * HEAD_PAD:(h + 1) * HEAD_PAD] = (roped * Q_PRESCALE).astype(BF16)
    c_ref[...] = _rms(dkv_ref[...], kvn_ref[...])
    akr = akr_ref[...]
    y = _rope_block(akr, cos_t, sin_a, sin_b)
    lane = lax.broadcasted_iota(jnp.int32, y.shape, 1)
    rope_lane = (lane >= ROPE_LO) & (lane < ROPE_LO + MLA_ROPE)
    kpe_ref[...] = jnp.where(rope_lane, y, 0.0).astype(BF16)
    kr_ref[...] = y[:, ROPE_LO:ROPE_LO + MLA_ROPE]
    xa = _dot(akr.astype(BF16), w2_ref[...]) + b2_ref[...]
    log_sig = jnp.minimum(xa, 0.0) - jnp.log(1.0 + jnp.exp(-jnp.abs(xa)))
    loga_ref[...] = log_sig * (1.0 / GLA_TAU)


def ab_proj(z, cols, tabs, row0, rows, q_norm, kv_norm, wuq_pad, w2_pad, b2, *, name):
    tab_rows = tabs[0].shape[0]
    tm = _row_tile(math.gcd(math.gcd(row0, rows) if row0 else rows, tab_rows), 512)
    off = row0 // tm
    ntab = tab_rows // tm
    dq0, dkv0, akr0 = cols
    ql, kvl = q_norm.shape[0], kv_norm.shape[0]
    nq = wuq_pad.shape[1]
    gw = w2_pad.shape[1]
    out_row = lambda w: pl.BlockSpec((tm, w), lambda i: (i, 0))
    tab = pl.BlockSpec((tm, LANES), lambda i: (i % ntab, 0))
    return pl.pallas_call(
        _ab_proj_kernel,
        grid=(rows // tm,),
        in_specs=[
            pl.BlockSpec((tm, ql), lambda i: (i + off, dq0 // ql)),
            pl.BlockSpec((tm, kvl), lambda i: (i + off, dkv0 // kvl)),
            pl.BlockSpec((tm, LANES), lambda i: (i + off, akr0 // LANES)),
            tab, tab, tab,
            _const_spec((1, ql)), _const_spec((1, kvl)),
            _const_spec((ql, nq)), _const_spec((LANES, gw)), _const_spec((1, gw)),
        ],
        out_specs=[out_row(gw), out_row(nq), out_row(kvl), out_row(MLA_ROPE), out_row(LANES)],
        out_shape=[
            jax.ShapeDtypeStruct((rows, gw), F32),
            jax.ShapeDtypeStruct((rows, nq), BF16),
            jax.ShapeDtypeStruct((rows, kvl), F32),
            jax.ShapeDtypeStruct((rows, MLA_ROPE), F32),
            jax.ShapeDtypeStruct((rows, LANES), BF16),
        ],
        compiler_params=_cparams(("parallel",)),
        name=name,
    )(z, z, z, *tabs, q_norm.reshape(1, ql), kv_norm.reshape(1, kvl), wuq_pad, w2_pad, b2.reshape(1, gw))


def _seg_cumsum(x, rg, group):
    k = 1
    while k < group:
        x = x + jnp.where(rg >= k, pltpu.roll(x, k, 0), 0.0)
        k *= 2
    return x


def _seg_first(x, rg, group):
    y = jnp.where(rg == 0, x, 0.0)
    k = 1
    while k < group:
        y = y + pltpu.roll(y, k, 0)
        k *= 2
    return y


def _seg_last(x, rg, group):
    n = x.shape[0]
    y = jnp.where(rg == group - 1, x, 0.0)
    k = 1
    while k < group:
        y = y + pltpu.roll(y, n - k, 0)
        k *= 2
    return y


def _gla_out(o, gate, gn):
    return _rms(o, gn) * (gate * _sigmoid(gate))


def _gla_prompt_kernel(q_ref, k_ref, v_ref, gate_ref, la_ref, gn_ref, a_hbm, o_ref, sf_ref, s_ref, *, nchunk):
    del a_hbm
    c = GLA_CHUNK
    npair = GLA_HEADS // 2
    j = pl.program_id(1)

    @pl.when(j == 0)
    def _():
        s_ref[...] = jnp.zeros_like(s_ref)

    gn = gn_ref[...]
    row = lax.broadcasted_iota(jnp.int32, (c, LANES), 0)
    lane = lax.broadcasted_iota(jnp.int32, (c, LANES), 1)
    head_lane = [lane < GLA_DK, lane >= GLA_DK]
    rg = row & (GLA_SUB - 1)
    t_i = lax.broadcasted_iota(jnp.int32, (c, c), 0)
    s_i = lax.broadcasted_iota(jnp.int32, (c, c), 1)
    sub_shift = GLA_SUB.bit_length() - 1
    tb, sb = t_i >> sub_shift, s_i >> sub_shift
    m_diag = (tb == sb) & (s_i <= t_i)
    m_next = (tb == sb + 1) & ((sb & 1) == 0)
    m_half = (t_i >= c // 2) & (s_i < c // 2)

    def chunk(ci, carry):
        r0 = pl.multiple_of(ci * c, c)
        for p in range(npair):
            qk = slice(p * LANES, (p + 1) * LANES)
            q = q_ref[pl.ds(r0, c), qk] * (GLA_DK ** -0.5)
            k = k_ref[pl.ds(r0, c), qk]
            b = _seg_cumsum(la_ref[pl.ds(r0, c), qk], row, c)
            r_sub = _seg_first(b, rg, GLA_SUB)
            r_next = pltpu.roll(r_sub, c - GLA_SUB, 0)
            b_half = b[c // 2:c // 2 + 1, :]
            q_sub = q * jnp.exp(b - r_sub)
            k_sub = k * jnp.exp(r_sub - b)
            k_next = k * jnp.exp(jnp.minimum(r_next - b, 0.0))
            q_half = q * jnp.exp(jnp.minimum(b - b_half, 0.0))
            k_half = k * jnp.exp(jnp.minimum(b_half - b, 0.0))
            q_state = q * jnp.exp(b)
            k_t = k.T
            b_t = b.T
            b_last = b_t[:, c - 1:c]
            k_upd = k_t * jnp.exp(b_last - b_t)
            s_decay = jnp.exp(b_last)
            s_pair = s_ref[p * LANES:(p + 1) * LANES, :]
            for h in range(2):
                hm = head_lane[h]
                z = lambda a: jnp.where(hm, a, 0.0)
                attn = (jnp.where(m_diag, _dot_nt(z(q_sub), k_sub), 0.0)
                        + jnp.where(m_next, _dot_nt(z(q_sub), k_next), 0.0)
                        + jnp.where(m_half, _dot_nt(z(q_half), k_half), 0.0))
                vcol = slice((2 * p + h) * GLA_DV, (2 * p + h + 1) * GLA_DV)
                vh = v_ref[pl.ds(r0, c), vcol]
                o = _dot(z(q_state), s_pair) + _dot(attn, vh)
                o_ref[pl.ds(r0, c), vcol] = _gla_out(o, gate_ref[pl.ds(r0, c), vcol], gn).astype(BF16)
                lo, hi = h * GLA_DK, (h + 1) * GLA_DK
                srow = slice(p * LANES + lo, p * LANES + hi)
                s_ref[srow, :] = s_decay[lo:hi] * s_pair[lo:hi] + _dot(k_upd[lo:hi], vh)
        return carry

    lax.fori_loop(0, nchunk, chunk, 0)

    @pl.when(j == pl.num_programs(1) - 1)
    def _():
        for h in range(GLA_HEADS):
            sf_ref[0, h] = s_ref[h * GLA_DK:(h + 1) * GLA_DK, :]


def gla_prompt(z, loga, gla_norm, cols, a, bp, s, *, name):
    q0, k0, v0, g0 = cols
    tb = _row_tile(s, 512)
    assert tb % GLA_CHUNK == 0
    nj = s // tb
    qk_w, v_w = GLA_HEADS * GLA_DK, GLA_HEADS * GLA_DV
    rowblk = lambda b, j: b * nj + j
    return pl.pallas_call(
        functools.partial(_gla_prompt_kernel, nchunk=tb // GLA_CHUNK),
        grid=(bp, nj),
        in_specs=[
            pl.BlockSpec((tb, qk_w), lambda b, j: (rowblk(b, j), q0 // qk_w)),
            pl.BlockSpec((tb, qk_w), lambda b, j: (rowblk(b, j), k0 // qk_w)),
            pl.BlockSpec((tb, v_w), lambda b, j: (rowblk(b, j), v0 // v_w)),
            pl.BlockSpec((tb, v_w), lambda b, j: (rowblk(b, j), g0 // v_w)),
            pl.BlockSpec((tb, qk_w), lambda b, j: (rowblk(b, j), 0)),
            pl.BlockSpec((1, GLA_DV), lambda b, j: (0, 0)),
            _ANY,
        ],
        out_specs=[
            pl.BlockSpec((tb, v_w), lambda b, j: (rowblk(b, j), 0)),
            pl.BlockSpec((1, GLA_HEADS, GLA_DK, GLA_DV), lambda b, j: (b, 0, 0, 0)),
        ],
        out_shape=[
            jax.ShapeDtypeStruct(a.shape, a.dtype),
            jax.ShapeDtypeStruct((bp, GLA_HEADS, GLA_DK, GLA_DV), F32),
        ],
        scratch_shapes=[pltpu.VMEM((GLA_HEADS * GLA_DK, GLA_DV), F32)],
        input_output_aliases={6: 0},
        compiler_params=_cparams(("parallel", "arbitrary")),
        name=name,
    )(z, z, z, z, loga, gla_norm.reshape(1, GLA_DV), a)


def _gla_sample_kernel(q_ref, k_ref, v_ref, gate_ref, la_ref, gn_ref, s0_ref, a_hbm, o_ref, sn_ref, *, t, nseq):
    del a_hbm
    n = nseq * t
    npair = GLA_HEADS // 2
    gn = gn_ref[...]
    row = lax.broadcasted_iota(jnp.int32, (n, LANES), 0)
    lane = lax.broadcasted_iota(jnp.int32, (n, LANES), 1)
    head_lane = [lane < GLA_DK, lane >= GLA_DK]
    t_shift, lane_shift = t.bit_length() - 1, LANES.bit_length() - 1
    rg = row & (t - 1)
    t_i = lax.broadcasted_iota(jnp.int32, (n, n), 0)
    s_i = lax.broadcasted_iota(jnp.int32, (n, n), 1)
    m_seq = ((t_i >> t_shift) == (s_i >> t_shift)) & (s_i <= t_i)
    wide = nseq * LANES
    w_row = lax.broadcasted_iota(jnp.int32, (n, wide), 0)
    w_col = lax.broadcasted_iota(jnp.int32, (n, wide), 1)
    own_seq = (w_row >> t_shift) == (w_col >> lane_shift)
    own_last = own_seq & ((w_row & (t - 1)) == t - 1)
    w_head = [(w_col & GLA_DK) == 0, (w_col & GLA_DK) != 0]
    srow_head1 = (lax.broadcasted_iota(jnp.int32, (wide, GLA_DV), 0) & GLA_DK) != 0
    tile = lambda a: jnp.concatenate([a] * nseq, axis=1)

    for p in range(npair):
        qk = slice(p * LANES, (p + 1) * LANES)
        q = q_ref[:, qk] * (GLA_DK ** -0.5)
        k = k_ref[:, qk]
        b = _seg_cumsum(la_ref[:, qk], rg, t)
        r_first = _seg_first(b, rg, t)
        r_last = _seg_last(b, rg, t)
        q_sub = q * jnp.exp(b - r_first)
        k_sub = k * jnp.exp(r_first - b)
        q_state = q * jnp.exp(b)
        k_upd = k * jnp.exp(r_last - b)
        s_all = s0_ref[:, 2 * p:2 * p + 2].reshape(wide, GLA_DV)
        q_wide = jnp.where(own_seq, tile(q_state), 0.0)
        k_wide_t = jnp.where(own_seq, tile(k_upd), 0.0).T
        decay = jnp.exp(jnp.sum(jnp.where(own_last, tile(b), 0.0).T, axis=1, keepdims=True))
        upd = []
        for h in range(2):
            attn = jnp.where(m_seq, _dot_nt(jnp.where(head_lane[h], q_sub, 0.0), k_sub), 0.0)
            vcol = slice((2 * p + h) * GLA_DV, (2 * p + h + 1) * GLA_DV)
            vh = v_ref[:, vcol]
            o = _dot(jnp.where(w_head[h], q_wide, 0.0), s_all) + _dot(attn, vh)
            o_ref[:, vcol] = _gla_out(o, gate_ref[:, vcol], gn).astype(BF16)
            upd.append(_dot(k_wide_t, vh))
        s_new = decay * s_all + jnp.where(srow_head1, upd[1], upd[0])
        sn_ref[:, 2 * p:2 * p + 2] = s_new.reshape(nseq, 2, GLA_DK, GLA_DV)


def gla_sample(z, loga, gla_norm, cols, s0, a, row0, bd, t, *, name):
    q0, k0, v0, g0 = cols
    nseq = 16 if bd % 16 == 0 else bd
    n = nseq * t
    assert n % 8 == 0 and row0 % n == 0 and t & (t - 1) == 0
    qk_w, v_w = GLA_HEADS * GLA_DK, GLA_HEADS * GLA_DV
    off = row0 // n
    return pl.pallas_call(
        functools.partial(_gla_sample_kernel, t=t, nseq=nseq),
        grid=(bd // nseq,),
        in_specs=[
            pl.BlockSpec((n, qk_w), lambda i: (i + off, q0 // qk_w)),
            pl.BlockSpec((n, qk_w), lambda i: (i + off, k0 // qk_w)),
            pl.BlockSpec((n, v_w), lambda i: (i + off, v0 // v_w)),
            pl.BlockSpec((n, v_w), lambda i: (i + off, g0 // v_w)),
            pl.BlockSpec((n, qk_w), lambda i: (i, 0)),
            pl.BlockSpec((1, GLA_DV), lambda i: (0, 0)),
            pl.BlockSpec((nseq, GLA_HEADS, GLA_DK, GLA_DV), lambda i: (i, 0, 0, 0)),
            _ANY,
        ],
        out_specs=[
            pl.BlockSpec((n, v_w), lambda i: (i + off, 0)),
            pl.BlockSpec((nseq, GLA_HEADS, GLA_DK, GLA_DV), lambda i: (i, 0, 0, 0)),
        ],
        out_shape=[
            jax.ShapeDtypeStruct(a.shape, a.dtype),
            jax.ShapeDtypeStruct((bd, GLA_HEADS, GLA_DK, GLA_DV), F32),
        ],
        input_output_aliases={7: 0},
        compiler_params=_cparams(("parallel",)),
        name=name,
    )(z, z, z, z, loga, gla_norm.reshape(1, GLA_DV), s0, a)


def _mla_self_kernel(q_ref, c_ref, kpe_ref, w_ref, a_hbm, o_ref, k_scr, v_scr, *, tq):
    del a_hbm
    i = pl.program_id(2)

    @pl.when(i == 0)
    def _():
        kv = _dot(c_ref[...].astype(BF16), w_ref[0])
        kpe = kpe_ref[...].astype(F32)
        for h in range(2):
            k_scr[h] = (kv[:, h * HEAD_PAD:(h + 1) * HEAD_PAD] + kpe).astype(BF16)
            v_scr[h] = kv[:, (2 + h) * HEAD_PAD:(3 + h) * HEAD_PAD].astype(BF16)

    t_i = lax.broadcasted_iota(jnp.int32, (tq, tq), 0)
    s_i = lax.broadcasted_iota(jnp.int32, (tq, tq), 1)
    causal = s_i <= t_i
    qs = [q_ref[:, h * HEAD_PAD:(h + 1) * HEAD_PAD] for h in range(2)]

    def step(kb, carry, masked):
        r0 = pl.multiple_of(kb * tq, tq)
        new = []
        for h in range(2):
            m, l, acc = carry[h]
            s = _dot_nt(qs[h], k_scr[h, pl.ds(r0, tq), :])
            if masked:
                s = jnp.where(causal, s, -jnp.inf)
            m_new = jnp.maximum(m, jnp.max(s, axis=1, keepdims=True))
            alpha = jnp.exp2(m - m_new)
            p = jnp.exp2(s - m_new)
            l = alpha * l + jnp.sum(p, axis=1, keepdims=True)
            acc = alpha * acc + _dot(p.astype(BF16), v_scr[h, pl.ds(r0, tq), :])
            new.append((m_new, l, acc))
        return tuple(new)

    init1 = (jnp.full((tq, 1), -jnp.inf, F32), jnp.zeros((tq, 1), F32), jnp.zeros((tq, HEAD_PAD), F32))
    carry = lax.fori_loop(0, i, functools.partial(step, masked=False), (init1, init1))
    (_, l0, acc0), (_, l1, acc1) = step(i, carry, True)
    o_ref[...] = (acc0 / l0 + acc1 / l1).astype(BF16)


def mla_self(qp, c_new, kpe, wkv_pair, a, col0, bp, s, *, name):
    tq = _row_tile(s, 512)
    nq = s // tq
    npair = MLA_HEADS // 2
    kvl = c_new.shape[1]
    ow = 2 * MLA_V
    assert col0 % ow == 0
    return pl.pallas_call(
        functools.partial(_mla_self_kernel, tq=tq),
        grid=(bp, npair, nq),
        in_specs=[
            pl.BlockSpec((tq, 2 * HEAD_PAD), lambda b, p, i: (b * nq + i, p)),
            pl.BlockSpec((s, kvl), lambda b, p, i: (b, 0)),
            pl.BlockSpec((s, LANES), lambda b, p, i: (b, 0)),
            pl.BlockSpec((1, kvl, 4 * HEAD_PAD), lambda b, p, i: (p, 0, 0)),
            _ANY,
        ],
        out_specs=pl.BlockSpec((tq, ow), lambda b, p, i: (b * nq + i, col0 // ow + p)),
        out_shape=jax.ShapeDtypeStruct(a.shape, a.dtype),
        scratch_shapes=[pltpu.VMEM((2, s, HEAD_PAD), BF16), pltpu.VMEM((2, s, HEAD_PAD), BF16)],
        input_output_aliases={4: 0},
        compiler_params=_cparams(("parallel", "parallel", "arbitrary")),
        name=name,
    )(qp, c_new, kpe, wkv_pair, a)


def _q_absorb_kernel(q_ref, w_ref, o_ref):
    o_ref[0] = _dot(q_ref[...], w_ref[0])


def q_absorb(qp, wq_abs, *, name):
    rows = qp.shape[0]
    wout = wq_abs.shape[2]
    return pl.pallas_call(
        _q_absorb_kernel,
        grid=(MLA_HEADS,),
        in_specs=[
            pl.BlockSpec((rows, HEAD_PAD), lambda h: (0, h)),
            pl.BlockSpec((1, HEAD_PAD, wout), lambda h: (h, 0, 0)),
        ],
        out_specs=pl.BlockSpec((1, rows, wout), lambda h: (h, 0, 0)),
        out_shape=jax.ShapeDtypeStruct((MLA_HEADS, rows, wout), F32),
        compiler_params=_cparams(("parallel",)),
        name=name,
    )(qp, wq_abs)


def _mla_cached_kernel(pt_ref, q_ref, cn_ref, krn_ref, ckv_hbm, ckr_hbm, o_ref, kv_buf, kr_buf, sem,
                       *, layer, npages, page, t, kvl, nchunk):
    b = pl.program_id(0)
    nb = pl.num_programs(0)
    slot = b % 2
    cw = npages // nchunk * page

    def copies(seq, sl, p):
        pg = pt_ref[seq, p]
        return (pltpu.make_async_copy(ckv_hbm.at[layer, pg], kv_buf.at[sl, pl.ds(p * page, page)], sem.at[0, sl]),
                pltpu.make_async_copy(ckr_hbm.at[layer, pg], kr_buf.at[sl, :, pl.ds(p * page, page)], sem.at[1, sl]))

    def start_all(seq, sl):
        for p in range(npages):
            for cp in copies(seq, sl, p):
                cp.start()

    @pl.when(b == 0)
    def _():
        start_all(0, 0)

    @pl.when(b + 1 < nb)
    def _():
        start_all(b + 1, 1 - slot)

    q = q_ref[0]
    q_lat = q[:, :kvl]
    q_pe = q[:, kvl:kvl + MLA_ROPE]
    c_new = cn_ref[0]
    kr_new = krn_ref[0]
    n = q.shape[0]
    tok = lax.broadcasted_iota(jnp.int32, (n, 1), 0) & (t - 1)
    s_new = []
    for jn in range(t):
        sj = (jnp.sum(q_lat * c_new[jn:jn + 1, :], axis=1, keepdims=True)
              + jnp.sum(q_pe * kr_new[jn:jn + 1, :], axis=1, keepdims=True))
        s_new.append(jnp.where(tok >= jn, sj, -jnp.inf))
    m = s_new[0]
    for sj in s_new[1:]:
        m = jnp.maximum(m, sj)
    l = jnp.zeros((n, 1), F32)
    o = jnp.zeros((n, kvl), F32)
    for sj, jn in zip(s_new, range(t)):
        pj = jnp.exp2(sj - m)
        l = l + pj
        o = o + pj * c_new[jn:jn + 1, :]
    for p in range(npages):
        for cp in copies(b, slot, p):
            cp.wait()
    q_rows = jnp.concatenate([q_lat.astype(BF16), jnp.zeros((LANES - n, kvl), BF16)], axis=0)
    for ch in range(nchunk):
        past_c = kv_buf[slot, ch * cw:(ch + 1) * cw, :].astype(BF16)
        past_rt = kr_buf[slot, :, ch * cw:(ch + 1) * cw]
        s = _dot_nt(past_c, q_rows).T[:n] + _dot(q_pe, past_rt)
        m_new = jnp.maximum(m, jnp.max(s, axis=1, keepdims=True))
        alpha = jnp.exp2(m - m_new)
        p_c = jnp.exp2(s - m_new)
        l = alpha * l + jnp.sum(p_c, axis=1, keepdims=True)
        o = alpha * o + _dot(p_c.astype(BF16), past_c)
        m = m_new
    o_ref[0] = o / l


def mla_cached(page_table, q_abs, c_new_s, kr_new_s, cache_kv, cache_kr_t, layer, *, name):
    bd, npages = page_table.shape
    page, kvl = cache_kv.shape[2], cache_kv.shape[3]
    rope = cache_kr_t.shape[2]
    t = c_new_s.shape[1]
    n, qw = q_abs.shape[1], q_abs.shape[2]
    past = npages * page
    nchunk = 4 if npages % 4 == 0 else 1
    assert t & (t - 1) == 0 and n <= LANES
    grid_spec = pltpu.PrefetchScalarGridSpec(
        num_scalar_prefetch=1,
        grid=(bd,),
        in_specs=[
            pl.BlockSpec((1, n, qw), lambda b, pt: (b, 0, 0)),
            pl.BlockSpec((1, t, kvl), lambda b, pt: (b, 0, 0)),
            pl.BlockSpec((1, t, rope), lambda b, pt: (b, 0, 0)),
            _ANY,
            _ANY,
        ],
        out_specs=pl.BlockSpec((1, n, kvl), lambda b, pt: (b, 0, 0)),
        scratch_shapes=[
            pltpu.VMEM((2, past, kvl), F32),
            pltpu.VMEM((2, rope, past), F32),
            pltpu.SemaphoreType.DMA((2, 2)),
        ],
    )
    return pl.pallas_call(
        functools.partial(_mla_cached_kernel, layer=layer, npages=npages, page=page, t=t, kvl=kvl, nchunk=nchunk),
        grid_spec=grid_spec,
        out_shape=jax.ShapeDtypeStruct((bd, n, kvl), F32),
        compiler_params=_cparams(("arbitrary",)),
        name=name,
    )(page_table, q_abs, c_new_s, kr_new_s, cache_kv, cache_kr_t)


def _v_up_kernel(o_ref, w_ref, a_hbm, out_ref):
    del a_hbm
    out_ref[...] = (_dot(o_ref[0].astype(BF16), w_ref[0, 0]) + _dot(o_ref[1].astype(BF16), w_ref[0, 1])).astype(BF16)


def v_up(o_lat, wv_pair, a, row0, col0, *, name):
    h, rows, kvl = o_lat.shape
    ow = 2 * MLA_V
    assert row0 % rows == 0 and col0 % ow == 0
    return pl.pallas_call(
        _v_up_kernel,
        grid=(h // 2,),
        in_specs=[
            pl.BlockSpec((2, rows, kvl), lambda p: (p, 0, 0)),
            pl.BlockSpec((1, 2, kvl, ow), lambda p: (p, 0, 0, 0)),
            _ANY,
        ],
        out_specs=pl.BlockSpec((rows, ow), lambda p: (row0 // rows, col0 // ow + p)),
        out_shape=jax.ShapeDtypeStruct(a.shape, a.dtype),
        input_output_aliases={2: 0},
        compiler_params=_cparams(("parallel",)),
        name=name,
    )(o_lat, wv_pair, a)


CONV_PAD = 32
CONV_SUB = 64
CONV_LANES = 256


def _ln_swish(y, lg, lb):
    mu = jnp.mean(y, axis=-1, keepdims=True)
    yc = y - mu
    var = jnp.mean(yc * yc, axis=-1, keepdims=True)
    yn = yc * lax.rsqrt(var + EPS) * lg + lb
    return yn * _sigmoid(yn)


def _conv_prompt_kernel(u_ref, halo_ref, x_ref, wdw_ref, bdw_ref, lg_ref, lb_ref, w2_ref, b2_ref,
                        o_ref, st_ref, ext_ref, sh_ref, h_ref, *, tt, cw):
    hist = cw - 1
    lo = CONV_PAD - hist
    d = u_ref.shape[1]
    j = pl.program_id(1)

    @pl.when(j == 0)
    def _():
        ext_ref[0:CONV_PAD, :] = jnp.zeros((CONV_PAD, d), F32)

    @pl.when(j > 0)
    def _():
        ext_ref[0:CONV_PAD, :] = halo_ref[...]

    ext_ref[CONV_PAD:CONV_PAD + tt, :] = u_ref[...]
    st_ref[0] = ext_ref[CONV_PAD + tt - hist:CONV_PAD + tt, :]
    span = tt + CONV_PAD - SUBLANES
    for r in range(1, SUBLANES):
        for c0 in range(0, d, CONV_LANES):
            sh_ref[r - 1, 0:span, c0:c0 + CONV_LANES] = ext_ref[r:r + span, c0:c0 + CONV_LANES]

    def tile(ri, carry):
        r0 = pl.multiple_of(ri * CONV_SUB, CONV_SUB)
        for c0 in range(0, d, CONV_LANES):
            cs = slice(c0, c0 + CONV_LANES)
            acc = jnp.broadcast_to(bdw_ref[:, cs], (CONV_SUB, CONV_LANES))
            for w in range(cw):
                r, a = (w + lo) % SUBLANES, (w + lo) // SUBLANES
                rows = pl.ds(r0 + a * SUBLANES, CONV_SUB)
                src = ext_ref[rows, cs] if r == 0 else sh_ref[r - 1, rows, cs]
                acc = acc + src * wdw_ref[w:w + 1, cs]
            h_ref[pl.ds(r0, CONV_SUB), cs] = acc
        y = h_ref[pl.ds(r0, CONV_SUB), :]
        h_ref[pl.ds(r0, CONV_SUB), :] = _ln_swish(y, lg_ref[...], lb_ref[...])
        return carry

    lax.fori_loop(0, tt // CONV_SUB, tile, 0)
    o_ref[...] = x_ref[...] + b2_ref[...] + _dot(h_ref[...].astype(BF16), w2_ref[...])


def conv_prompt(u, x, w_dw, b_dw, ln_g, ln_b, w2_bf16, b2, bp, s, *, name):
    r, d = x.shape
    cw = w_dw.shape[0]
    hist = cw - 1
    tt = _row_tile(s, 512)
    assert hist <= CONV_PAD and tt % CONV_SUB == 0 and tt % CONV_PAD == 0 and d % CONV_LANES == 0
    nj = s // tt
    hb = tt // CONV_PAD
    blk = pl.BlockSpec((tt, d), lambda b, j: (b * nj + j, 0))
    return pl.pallas_call(
        functools.partial(_conv_prompt_kernel, tt=tt, cw=cw),
        grid=(bp, nj),
        in_specs=[
            blk,
            pl.BlockSpec((CONV_PAD, d), lambda b, j: (jnp.maximum((b * nj + j) * hb - 1, 0), 0)),
            blk,
            _const_spec((cw, d)), _const_spec((1, d)), _const_spec((1, d)), _const_spec((1, d)),
            _const_spec((d, d)), _const_spec((1, d)),
        ],
        out_specs=[blk, pl.BlockSpec((1, hist, d), lambda b, j: (b, 0, 0))],
        out_shape=[jax.ShapeDtypeStruct((r, d), F32), jax.ShapeDtypeStruct((bp, hist, d), F32)],
        scratch_shapes=[
            pltpu.VMEM((tt + CONV_PAD, d), F32),
            pltpu.VMEM((SUBLANES - 1, tt + CONV_PAD, d), F32),
            pltpu.VMEM((tt, d), F32),
        ],
        input_output_aliases={2: 0},
        compiler_params=_cparams(("parallel", "arbitrary")),
        name=name,
    )(u, u, x, w_dw, b_dw.reshape(1, d), ln_g.reshape(1, d), ln_b.reshape(1, d), w2_bf16, b2.reshape(1, d))


def _conv_sample_kernel(u_ref, buf_ref, x_ref, wdw_ref, bdw_ref, lg_ref, lb_ref, w2_ref, b2_ref,
                        o_ref, st_ref, ext_ref, h_ref, *, t, bb, cw):
    hist = cw - 1
    lo = CONV_PAD - hist
    for i in range(bb):
        ext_ref[i, lo:CONV_PAD, :] = buf_ref[i]
        ext_ref[i, CONV_PAD:CONV_PAD + t, :] = u_ref[i * t:(i + 1) * t, :]
        st_ref[i] = ext_ref[i, CONV_PAD + t - hist:CONV_PAD + t, :]
        acc = jnp.broadcast_to(bdw_ref[...], (t, bdw_ref.shape[1]))
        for w in range(cw):
            acc = acc + ext_ref[i, lo + w:lo + w + t, :] * wdw_ref[w:w + 1, :]
        h_ref[i * t:(i + 1) * t, :] = _ln_swish(acc, lg_ref[...], lb_ref[...])
    o_ref[...] = x_ref[...] + b2_ref[...] + _dot(h_ref[...].astype(BF16), w2_ref[...])


def conv_sample(u, buf, x, w_dw, b_dw, ln_g, ln_b, w2_bf16, b2, row0, bd, t, *, name):
    r, d = x.shape
    cw = w_dw.shape[0]
    hist = cw - 1
    bb = 32 if bd % 32 == 0 else bd
    rows = bb * t
    assert hist <= CONV_PAD and rows % 8 == 0 and row0 % rows == 0
    off = row0 // rows
    blk = pl.BlockSpec((rows, d), lambda i: (i + off, 0))
    ext_rows = -(-(CONV_PAD + t) // SUBLANES) * SUBLANES
    return pl.pallas_call(
        functools.partial(_conv_sample_kernel, t=t, bb=bb, cw=cw),
        grid=(bd // bb,),
        in_specs=[
            blk,
            pl.BlockSpec((bb, hist, d), lambda i: (i, 0, 0)),
            blk,
            _const_spec((cw, d)), _const_spec((1, d)), _const_spec((1, d)), _const_spec((1, d)),
            _const_spec((d, d)), _const_spec((1, d)),
        ],
        out_specs=[blk, pl.BlockSpec((bb, hist, d), lambda i: (i, 0, 0))],
        out_shape=[jax.ShapeDtypeStruct((r, d), F32), jax.ShapeDtypeStruct((bd, hist, d), F32)],
        scratch_shapes=[pltpu.VMEM((bb, ext_rows, d), F32), pltpu.VMEM((rows, d), F32)],
        input_output_aliases={2: 0},
        compiler_params=_cparams(("parallel",)),
        name=name,
    )(u, buf, x, w_dw, b_dw.reshape(1, d), ln_g.reshape(1, d), ln_b.reshape(1, d), w2_bf16, b2.reshape(1, d))


def _rope_tables(pos):
    inv = jnp.power(ROPE_THETA, -jnp.arange(0, MLA_ROPE, 2, dtype=F32) / MLA_ROPE)
    ang = pos.astype(F32)[:, None] * inv[None, :]
    cos, sin = jnp.cos(ang), jnp.sin(ang)
    n = pos.shape[0]
    lo = ROPE_LO
    cos_t = jnp.ones((n, LANES), F32).at[:, lo:lo + MLA_ROPE].set(jnp.concatenate([cos, cos], axis=1))
    sin_a = jnp.zeros((n, LANES), F32).at[:, lo:lo + ROPE_HALF].set(-sin)
    sin_b = jnp.zeros((n, LANES), F32).at[:, lo + ROPE_HALF:lo + MLA_ROPE].set(sin)
    return cos_t, sin_a, sin_b


def _layout_w_in(w_in, gla_qk, gla_v, q_lora, kv_lora):
    d = w_in.shape[0]
    sp = [0, gla_qk, 2 * gla_qk, 2 * gla_qk + gla_v, 2 * gla_qk + gla_v + GLA_RANK]
    sp += [sp[-1] + gla_v, sp[-1] + gla_v + q_lora, sp[-1] + gla_v + q_lora + kv_lora]
    q, k, v, a, g, dq, dkv, kr = [w_in[:, lo:hi] for lo, hi in zip(sp, sp[1:] + [w_in.shape[1]])]
    last = jnp.zeros((d, LANES), w_in.dtype).at[:, :GLA_RANK].set(a).at[:, ROPE_LO:ROPE_LO + MLA_ROPE].set(kr)
    w = jnp.concatenate([q, k, v, g, dq, dkv, last], axis=1)
    cols = dict(q=0, k=gla_qk, v=2 * gla_qk, g=2 * gla_qk + gla_v, dq=2 * gla_qk + 2 * gla_v)
    cols["dkv"] = cols["dq"] + q_lora
    cols["akr"] = cols["dkv"] + kv_lora
    return w.astype(BF16), cols


def _layout_w_uq(w_uq):
    ql = w_uq.shape[0]
    w = w_uq.reshape(ql, MLA_HEADS, MLA_NOPE + MLA_ROPE)
    w = jnp.pad(w, ((0, 0), (0, 0), (0, HEAD_PAD - MLA_NOPE - MLA_ROPE)))
    return w.reshape(ql, MLA_HEADS * HEAD_PAD).astype(BF16)


def _layout_w_ukv(w_ukv):
    kvl = w_ukv.shape[0]
    w = w_ukv.reshape(kvl, MLA_HEADS, MLA_NOPE + MLA_V)
    wk = jnp.pad(w[..., :MLA_NOPE], ((0, 0), (0, 0), (0, HEAD_PAD - MLA_NOPE)))
    wv = w[..., MLA_NOPE:]
    wv_e = jnp.pad(wv[:, 0::2], ((0, 0), (0, 0), (0, MLA_V)))
    wv_o = jnp.pad(wv[:, 1::2], ((0, 0), (0, 0), (MLA_V, 0)))
    pair = jnp.stack([wk[:, 0::2], wk[:, 1::2], wv_e, wv_o], axis=2)
    pair = pair.transpose(1, 0, 2, 3).reshape(MLA_HEADS // 2, kvl, 4 * HEAD_PAD)
    w_uk_t = w[..., :MLA_NOPE].transpose(1, 2, 0)
    wq_abs = jnp.zeros((MLA_HEADS, HEAD_PAD, kvl + LANES), F32)
    wq_abs = wq_abs.at[:, :MLA_NOPE, :kvl].set(w_uk_t)
    wq_abs = wq_abs.at[:, ROPE_LO:ROPE_LO + MLA_ROPE, kvl:kvl + MLA_ROPE].set(jnp.eye(MLA_ROPE, dtype=F32))
    wv_h = wv.transpose(1, 0, 2)
    wv_pair = jnp.stack([jnp.pad(wv_h[0::2], ((0, 0), (0, 0), (0, MLA_V))),
                         jnp.pad(wv_h[1::2], ((0, 0), (0, 0), (MLA_V, 0)))], axis=1)
    return pair.astype(BF16), wq_abs.astype(BF16), wv_pair.astype(BF16)


def kernel(x_prompt, x_sample, cache_kv, cache_kr, state_gla, state_conv, page_table, norm_mix, norm_mlp,
           norm_final, w_in, w_gate_a2, b_gate_a, gla_norm, mla_q_norm, mla_kv_norm, w_uq, w_ukv, w_out_ab,
           w_pw1, b_pw1, w_dw, b_dw, conv_ln_g, conv_ln_b, w_pw2, b_pw2, w_up, w_down):
    bp, s, d = x_prompt.shape
    bd, t, _ = x_sample.shape
    rp, rs = bp * s, bd * t
    r = rp + rs
    depth = norm_mix.shape[0]
    page = cache_kv.shape[2]
    past = page_table.shape[1] * page
    gla_qk, gla_v = GLA_HEADS * GLA_DK, GLA_HEADS * GLA_DV
    mla_w = MLA_HEADS * MLA_V
    q_lora, kv_lora = mla_q_norm.shape[1], mla_kv_norm.shape[1]
    d_ff = w_up.shape[2]
    tf = 512 if d_ff % 512 == 0 else d_ff

    tabs_p = _rope_tables(jnp.arange(s))
    tabs_s = _rope_tables(jnp.tile(past + jnp.arange(t), bd))
    cache_kr_t = jnp.swapaxes(cache_kr, 2, 3)

    x = jnp.concatenate([x_prompt.reshape(rp, d), x_sample.reshape(rs, d)], axis=0)
    kv_p, kr_p, gla_p, conv_p, kv_s, kr_s, gla_s, conv_s = ([] for _ in range(8))
    for l in range(depth):
        i = l // 2
        if l % 2 == 0:
            w_in_l, cols = _layout_w_in(w_in[i], gla_qk, gla_v, q_lora, kv_lora)
            z = norm_matmul(x, norm_mix[l], w_in_l, jnp.zeros((w_in_l.shape[1],), F32), name=f"in_proj{l}")
            w2_pad = jnp.zeros((LANES, gla_qk), F32).at[:GLA_RANK].set(w_gate_a2[i]).astype(BF16)
            pcols = (cols["dq"], cols["dkv"], cols["akr"])
            pargs = (mla_q_norm[i], mla_kv_norm[i], _layout_w_uq(w_uq[i]), w2_pad, b_gate_a[i])
            loga_p, qp_p, c_p, krn_p, kpe_p = ab_proj(z, pcols, tabs_p, 0, rp, *pargs, name=f"ab_proj_prompt{l}")
            loga_s, qp_s, c_s, krn_s, _ = ab_proj(z, pcols, tabs_s, rp, rs, *pargs, name=f"ab_proj_sample{l}")
            gcols = (cols["q"], cols["k"], cols["v"], cols["g"])
            a = jnp.zeros((r, gla_v + mla_w), BF16)
            a, sg_p = gla_prompt(z, loga_p, gla_norm[i], gcols, a, bp, s, name=f"gla_prompt{l}")
            a, sg_s = gla_sample(z, loga_s, gla_norm[i], gcols, state_gla[i], a, rp, bd, t, name=f"gla_sample{l}")
            wkv_pair, wq_abs, wv_pair = _layout_w_ukv(w_ukv[i])
            a = mla_self(qp_p, c_p, kpe_p, wkv_pair, a, gla_v, bp, s, name=f"mla_self{l}")
            q_abs = q_absorb(qp_s, wq_abs, name=f"q_absorb{l}")
            q_abs = q_abs.reshape(MLA_HEADS, bd, t, -1).transpose(1, 0, 2, 3).reshape(bd, MLA_HEADS * t, -1)
            c_s3 = c_s.reshape(bd, t, kv_lora)
            krn_s3 = krn_s.reshape(bd, t, MLA_ROPE)
            o_lat = mla_cached(page_table, q_abs, c_s3, krn_s3, cache_kv, cache_kr_t, i, name=f"mla_cached{l}")
            o_lat = o_lat.reshape(bd, MLA_HEADS, t, kv_lora).transpose(1, 0, 2, 3).reshape(MLA_HEADS, rs, kv_lora)
            a = v_up(o_lat, wv_pair, a, rp, gla_v, name=f"v_up{l}")
            x = out_proj_residual(a, w_out_ab[i].astype(BF16), x, name=f"out_proj{l}")
            kv_p.append(c_p.reshape(bp, s, kv_lora)); kr_p.append(krn_p.reshape(bp, s, MLA_ROPE))
            kv_s.append(c_s3); kr_s.append(krn_s3)
            gla_p.append(sg_p); gla_s.append(sg_s)
        else:
            u = norm_matmul(x, norm_mix[l], w_pw1[i].astype(BF16), b_pw1[i], glu=True, name=f"pw1_glu{l}")
            cargs = (w_dw[i], b_dw[i], conv_ln_g[i], conv_ln_b[i], w_pw2[i].astype(BF16), b_pw2[i])
            x, st_p = conv_prompt(u, x, *cargs, bp, s, name=f"conv_prompt{l}")
            x, st_s = conv_sample(u, state_conv[i], x, *cargs, rp, bd, t, name=f"conv_sample{l}")
            conv_p.append(st_p); conv_s.append(st_s)
        wup3 = w_up[l].astype(BF16).reshape(d, d_ff // tf, tf).transpose(1, 0, 2)
        wdn3 = w_down[l].astype(BF16).reshape(d_ff // tf, tf, d)
        x = mlp_residual(x, norm_mlp[l], wup3, wdn3, name=f"mlp{l}")
    y_p = final_norm(x, norm_final, 0, rp, name="final_norm_prompt").reshape(bp, s, d)
    y_s = final_norm(x, norm_final, rp, rs, name="final_norm_sample").reshape(bd, t, d)
    return (y_p, y_s, jnp.stack(kv_p), jnp.stack(kr_p), jnp.stack(gla_p), jnp.stack(conv_p),
            jnp.stack(kv_s), jnp.stack(kr_s), jnp.stack(gla_s), jnp.stack(conv_s))
```

```python
import functools
import math

import jax
import jax.numpy as jnp
from jax import lax
from jax.experimental import pallas as pl
from jax.experimental.pallas import tpu as pltpu

F32 = jnp.float32
BF16 = jnp.bfloat16

GLA_HEADS = 4
GLA_DK = 64
GLA_DV = 128
GLA_RANK = 16
GLA_TAU = 16.0
GLA_CHUNK = 64
GLA_SUB = 16
MLA_HEADS = 8
MLA_NOPE = 64
MLA_ROPE = 32
MLA_V = 64
MLA_SCALE = (MLA_NOPE + MLA_ROPE) ** -0.5
Q_PRESCALE = MLA_SCALE * math.log2(math.e)
ROPE_THETA = 10000.0
EPS = 1e-6

LANES = 128
SUBLANES = 8
HEAD_PAD = 128
ROPE_LO = MLA_NOPE
ROPE_HALF = MLA_ROPE // 2
VMEM_LIMIT = 56 * 1024 * 1024


def _cparams(sem):
    return pltpu.CompilerParams(dimension_semantics=sem, vmem_limit_bytes=VMEM_LIMIT)


def _row_tile(n, target):
    best = None
    for t in range(8, min(n, target) + 1, 8):
        if n % t == 0:
            best = t
    assert best is not None, n
    return best


def _const_spec(shape):
    nd = len(shape)
    return pl.BlockSpec(shape, lambda *_: (0,) * nd, pipeline_mode=pl.Buffered(1))


def _layer_spec(shape, layer):
    nd = len(shape)
    return pl.BlockSpec((None,) + tuple(shape), lambda *_: (layer,) + (0,) * nd, pipeline_mode=pl.Buffered(1))


_ANY = pl.BlockSpec(memory_space=pl.ANY)


def _rms(x, g):
    ms = jnp.mean(x * x, axis=-1, keepdims=True)
    return x * lax.rsqrt(ms + EPS) * g


def _sigmoid(x):
    return 1.0 / (1.0 + jnp.exp(-x))


def _dot(a, b):
    return jnp.dot(a, b, preferred_element_type=F32)


def _dot_nt(a, b):
    return lax.dot_general(a, b, (((1,), (1,)), ((), ())), preferred_element_type=F32)


def _norm_matmul_kernel(x_ref, g_ref, w_ref, b_ref, o_ref, *, glu):
    h = _rms(x_ref[...], g_ref[...]).astype(BF16)
    u = _dot(h, w_ref[...]) + b_ref[...]
    if glu:
        half = u.shape[1] // 2
        u = u[:, :half] * _sigmoid(u[:, half:])
    o_ref[...] = u


def norm_matmul(x, g, w_bf16, layer, bias, *, glu=False, name):
    r, d = x.shape
    n = w_bf16.shape[2]
    n_out = n // 2 if glu else n
    tm = _row_tile(r, 512)
    return pl.pallas_call(
        functools.partial(_norm_matmul_kernel, glu=glu),
        grid=(r // tm,),
        in_specs=[
            pl.BlockSpec((tm, d), lambda i: (i, 0)),
            _const_spec((1, d)),
            _layer_spec((d, n), layer),
            _const_spec((1, n)),
        ],
        out_specs=pl.BlockSpec((tm, n_out), lambda i: (i, 0)),
        out_shape=jax.ShapeDtypeStruct((r, n_out), F32),
        compiler_params=_cparams(("parallel",)),
        name=name,
    )(x, g.reshape(1, d), w_bf16, bias.reshape(1, n))


def _mlp_kernel(x_ref, g_ref, wup_ref, wdn_ref, o_ref, h_ref):
    x = x_ref[...]
    h_ref[...] = _rms(x, g_ref[...]).astype(BF16)
    o_ref[...] = x

    nf, tf = wdn_ref.shape[0], wdn_ref.shape[1]

    def body(f, carry):
        c0 = pl.multiple_of(f * tf, tf)
        u = jnp.maximum(_dot(h_ref[...], wup_ref[:, pl.ds(c0, tf)]), 0.0)
        o_ref[...] += _dot((u * u).astype(BF16), wdn_ref[f])
        return carry

    lax.fori_loop(0, nf, body, 0, unroll=2 if nf % 2 == 0 else 1)


def mlp_residual(x, g, wup, wdn4, layer, *, name):
    r, d = x.shape
    _, nf, tf, _ = wdn4.shape
    tm = _row_tile(r, 512)
    return pl.pallas_call(
        _mlp_kernel,
        grid=(r // tm,),
        in_specs=[
            pl.BlockSpec((tm, d), lambda i: (i, 0)),
            _const_spec((1, d)),
            _layer_spec((d, nf * tf), layer),
            _layer_spec((nf, tf, d), layer),
        ],
        out_specs=pl.BlockSpec((tm, d), lambda i: (i, 0)),
        out_shape=jax.ShapeDtypeStruct((r, d), F32),
        scratch_shapes=[pltpu.VMEM((tm, d), BF16)],
        compiler_params=_cparams(("parallel",)),
        name=name,
    )(x, g.reshape(1, d), wup, wdn4)


def _out_proj_kernel(a_ref, w_ref, r_ref, o_ref):
    o_ref[...] = r_ref[...] + _dot(a_ref[...], w_ref[...])


def out_proj_residual(a_bf16, w_bf16, layer, res, *, name):
    r, k = a_bf16.shape
    d = w_bf16.shape[2]
    tm = _row_tile(r, 512)
    return pl.pallas_call(
        _out_proj_kernel,
        grid=(r // tm,),
        in_specs=[
            pl.BlockSpec((tm, k), lambda i: (i, 0)),
            _layer_spec((k, d), layer),
            pl.BlockSpec((tm, d), lambda i: (i, 0)),
        ],
        out_specs=pl.BlockSpec((tm, d), lambda i: (i, 0)),
        out_shape=jax.ShapeDtypeStruct((r, d), F32),
        compiler_params=_cparams(("parallel",)),
        name=name,
    )(a_bf16, w_bf16, res)


def _final_norm_kernel(x_ref, g_ref, o_ref):
    o_ref[...] = _rms(x_ref[...], g_ref[...])


def final_norm(x, g, row0, rows, *, name):
    d = x.shape[1]
    tm = _row_tile(math.gcd(row0, rows) if row0 else rows, 512)
    off = row0 // tm
    return pl.pallas_call(
        _final_norm_kernel,
        grid=(rows // tm,),
        in_specs=[pl.BlockSpec((tm, d), lambda i: (i + off, 0)), _const_spec((1, d))],
        out_specs=pl.BlockSpec((tm, d), lambda i: (i, 0)),
        out_shape=jax.ShapeDtypeStruct((rows, d), F32),
        compiler_params=_cparams(("parallel",)),
        name=name,
    )(x, g.reshape(1, d))


def _rope_block(x, cos_t, sin_a, sin_b):
    up = pltpu.roll(x, LANES - ROPE_HALF, 1)
    dn = pltpu.roll(x, ROPE_HALF, 1)
    return x * cos_t + up * sin_a + dn * sin_b


def _ab_proj_kernel(dq_ref, dkv_ref, akr_ref, cos_ref, sa_ref, sb_ref, qn_ref, kvn_ref,
                    wuq_ref, w2_ref, b2_ref, loga_ref, q_ref, c_ref, kr_ref, kpe_ref):
    cos_t, sin_a, sin_b = cos_ref[...], sa_ref[...], sb_ref[...]
    cq = _rms(dq_ref[...], qn_ref[...]).astype(BF16)
    qm = _dot(cq, wuq_ref[...])
    for h in range(MLA_HEADS):
        blk = qm[:, h * HEAD_PAD:(h + 1) * HEAD_PAD]
        roped = _rope_block(blk, cos_t, sin_a, sin_b)
        q_ref[:, h * HEAD_PAD:(h + 1) * HEAD_PAD] = (roped * Q_PRESCALE).astype(BF16)
    c_ref[...] = _rms(dkv_ref[...], kvn_ref[...])
    akr = akr_ref[...]
    y = _rope_block(akr, cos_t, sin_a, sin_b)
    lane = lax.broadcasted_iota(jnp.int32, y.shape, 1)
    rope_lane = (lane >= ROPE_LO) & (lane < ROPE_LO + MLA_ROPE)
    kpe_ref[...] = jnp.where(rope_lane, y, 0.0).astype(BF16)
    kr_ref[...] = y[:, ROPE_LO:ROPE_LO + MLA_ROPE]
    xa = _dot(akr.astype(BF16), w2_ref[...]) + b2_ref[...]
    log_sig = jnp.minimum(xa, 0.0) - jnp.log(1.0 + jnp.exp(-jnp.abs(xa)))
    loga_ref[...] = log_sig * (1.0 / GLA_TAU)


def ab_proj(z, cols, tabs, row0, rows, q_norm, kv_norm, wuq_pad, w2_pad, b2, *, name):
    tab_rows = tabs[0].shape[0]
    tm = _row_tile(math.gcd(math.gcd(row0, rows) if row0 else rows, tab_rows), 512)
    off = row0 // tm
    ntab = tab_rows // tm
    dq0, dkv0, akr0 = cols
    ql, kvl = q_norm.shape[0], kv_norm.shape[0]
    nq = wuq_pad.shape[1]
    gw = w2_pad.shape[1]
    out_row = lambda w: pl.BlockSpec((tm, w), lambda i: (i, 0))
    tab = pl.BlockSpec((tm, LANES), lambda i: (i % ntab, 0))
    return pl.pallas_call(
        _ab_proj_kernel,
        grid=(rows // tm,),
        in_specs=[
            pl.BlockSpec((tm, ql), lambda i: (i + off, dq0 // ql)),
            pl.BlockSpec((tm, kvl), lambda i: (i + off, dkv0 // kvl)),
            pl.BlockSpec((tm, LANES), lambda i: (i + off, akr0 // LANES)),
            tab, tab, tab,
            _const_spec((1, ql)), _const_spec((1, kvl)),
            _const_spec((ql, nq)), _const_spec((LANES, gw)), _const_spec((1, gw)),
        ],
        out_specs=[out_row(gw), out_row(nq), out_row(kvl), out_row(MLA_ROPE), out_row(LANES)],
        out_shape=[
            jax.ShapeDtypeStruct((rows, gw), F32),
            jax.ShapeDtypeStruct((rows, nq), BF16),
            jax.ShapeDtypeStruct((rows, kvl), F32),
            jax.ShapeDtypeStruct((rows, MLA_ROPE), F32),
            jax.ShapeDtypeStruct((rows, LANES), BF16),
        ],
        compiler_params=_cparams(("parallel",)),
        name=name,
    )(z, z, z, *tabs, q_norm.reshape(1, ql), kv_norm.reshape(1, kvl), wuq_pad, w2_pad, b2.reshape(1, gw))


def _seg_cumsum(x, rg, group):
    k = 1
    while k < group:
        x = x + jnp.where(rg >= k, pltpu.roll(x, k, 0), 0.0)
        k *= 2
    return x


def _seg_first(x, rg, group):
    y = jnp.where(rg == 0, x, 0.0)
    k = 1
    while k < group:
        y = y + pltpu.roll(y, k, 0)
        k *= 2
    return y


def _seg_last(x, rg, group):
    n = x.shape[0]
    y = jnp.where(rg == group - 1, x, 0.0)
    k = 1
    while k < group:
        y = y + pltpu.roll(y, n - k, 0)
        k *= 2
    return y


def _gla_out(o, gate, gn):
    return _rms(o, gn) * (gate * _sigmoid(gate))


def _gla_prompt_kernel(q_ref, k_ref, v_ref, gate_ref, la_ref, gn_ref, a_hbm, o_ref, sf_ref, s_ref, *, nchunk):
    del a_hbm
    c = GLA_CHUNK
    npair = GLA_HEADS // 2
    j = pl.program_id(1)

    @pl.when(j == 0)
    def _():
        s_ref[...] = jnp.zeros_like(s_ref)

    gn = gn_ref[...]
    row = lax.broadcasted_iota(jnp.int32, (c, LANES), 0)
    lane = lax.broadcasted_iota(jnp.int32, (c, LANES), 1)
    head_lane = [lane < GLA_DK, lane >= GLA_DK]
    rg = row & (GLA_SUB - 1)
    t_i = lax.broadcasted_iota(jnp.int32, (c, c), 0)
    s_i = lax.broadcasted_iota(jnp.int32, (c, c), 1)
    sub_shift = GLA_SUB.bit_length() - 1
    tb, sb = t_i >> sub_shift, s_i >> sub_shift
    m_diag = (tb == sb) & (s_i <= t_i)
    m_next = (tb == sb + 1) & ((sb & 1) == 0)
    m_half = (t_i >= c // 2) & (s_i < c // 2)

    def chunk(ci, carry):
        r0 = pl.multiple_of(ci * c, c)
        for p in range(npair):
            qk = slice(p * LANES, (p + 1) * LANES)
            q = q_ref[pl.ds(r0, c), qk] * (GLA_DK ** -0.5)
            k = k_ref[pl.ds(r0, c), qk]
            b = _seg_cumsum(la_ref[pl.ds(r0, c), qk], row, c)
            r_sub = _seg_first(b, rg, GLA_SUB)
            r_next = pltpu.roll(r_sub, c - GLA_SUB, 0)
            b_half = b[c // 2:c // 2 + 1, :]
            q_sub = q * jnp.exp(b - r_sub)
            k_sub = k * jnp.exp(r_sub - b)
            k_next = k * jnp.exp(jnp.minimum(r_next - b, 0.0))
            q_half = q * jnp.exp(jnp.minimum(b - b_half, 0.0))
            k_half = k * jnp.exp(jnp.minimum(b_half - b, 0.0))
            q_state = q * jnp.exp(b)
            k_t = k.T
            b_t = b.T
            b_last = b_t[:, c - 1:c]
            k_upd = k_t * jnp.exp(b_last - b_t)
            s_decay = jnp.exp(b_last)
            s_pair = s_ref[p * LANES:(p + 1) * LANES, :]
            for h in range(2):
                hm = head_lane[h]
                z = lambda a: jnp.where(hm, a, 0.0)
                attn = (jnp.where(m_diag, _dot_nt(z(q_sub), k_sub), 0.0)
                        + jnp.where(m_next, _dot_nt(z(q_sub), k_next), 0.0)
                        + jnp.where(m_half, _dot_nt(z(q_half), k_half), 0.0))
                vcol = slice((2 * p + h) * GLA_DV, (2 * p + h + 1) * GLA_DV)
                vh = v_ref[pl.ds(r0, c), vcol]
                o = _dot(z(q_state), s_pair) + _dot(attn, vh)
                o_ref[pl.ds(r0, c), vcol] = _gla_out(o, gate_ref[pl.ds(r0, c), vcol], gn).astype(BF16)
                lo, hi = h * GLA_DK, (h + 1) * GLA_DK
                srow = slice(p * LANES + lo, p * LANES + hi)
                s_ref[srow, :] = s_decay[lo:hi] * s_pair[lo:hi] + _dot(k_upd[lo:hi], vh)
        return carry

    lax.fori_loop(0, nchunk, chunk, 0, unroll=2 if nchunk % 2 == 0 else 1)

    @pl.when(j == pl.num_programs(1) - 1)
    def _():
        for h in range(GLA_HEADS):
            sf_ref[0, h] = s_ref[h * GLA_DK:(h + 1) * GLA_DK, :]


def gla_prompt(z, loga, gla_norm, cols, a, bp, s, *, name):
    q0, k0, v0, g0 = cols
    tb = _row_tile(s, 512)
    assert tb % GLA_CHUNK == 0
    nj = s // tb
    qk_w, v_w = GLA_HEADS * GLA_DK, GLA_HEADS * GLA_DV
    rowblk = lambda b, j: b * nj + j
    return pl.pallas_call(
        functools.partial(_gla_prompt_kernel, nchunk=tb // GLA_CHUNK),
        grid=(bp, nj),
        in_specs=[
            pl.BlockSpec((tb, qk_w), lambda b, j: (rowblk(b, j), q0 // qk_w)),
            pl.BlockSpec((tb, qk_w), lambda b, j: (rowblk(b, j), k0 // qk_w)),
            pl.BlockSpec((tb, v_w), lambda b, j: (rowblk(b, j), v0 // v_w)),
            pl.BlockSpec((tb, v_w), lambda b, j: (rowblk(b, j), g0 // v_w)),
            pl.BlockSpec((tb, qk_w), lambda b, j: (rowblk(b, j), 0)),
            pl.BlockSpec((1, GLA_DV), lambda b, j: (0, 0)),
            _ANY,
        ],
        out_specs=[
            pl.BlockSpec((tb, v_w), lambda b, j: (rowblk(b, j), 0)),
            pl.BlockSpec((1, GLA_HEADS, GLA_DK, GLA_DV), lambda b, j: (b, 0, 0, 0)),
        ],
        out_shape=[
            jax.ShapeDtypeStruct(a.shape, a.dtype),
            jax.ShapeDtypeStruct((bp, GLA_HEADS, GLA_DK, GLA_DV), F32),
        ],
        scratch_shapes=[pltpu.VMEM((GLA_HEADS * GLA_DK, GLA_DV), F32)],
        input_output_aliases={6: 0},
        compiler_params=_cparams(("parallel", "arbitrary")),
        name=name,
    )(z, z, z, z, loga, gla_norm.reshape(1, GLA_DV), a)


def _gla_sample_kernel(q_ref, k_ref, v_ref, gate_ref, la_ref, gn_ref, s0_ref, a_hbm, o_ref, sn_ref, *, t, nseq):
    del a_hbm
    n = nseq * t
    npair = GLA_HEADS // 2
    gn = gn_ref[...]
    row = lax.broadcasted_iota(jnp.int32, (n, LANES), 0)
    lane = lax.broadcasted_iota(jnp.int32, (n, LANES), 1)
    head_lane = [lane < GLA_DK, lane >= GLA_DK]
    t_shift, lane_shift = t.bit_length() - 1, LANES.bit_length() - 1
    rg = row & (t - 1)
    t_i = lax.broadcasted_iota(jnp.int32, (n, n), 0)
    s_i = lax.broadcasted_iota(jnp.int32, (n, n), 1)
    m_seq = ((t_i >> t_shift) == (s_i >> t_shift)) & (s_i <= t_i)
    wide = nseq * LANES
    w_row = lax.broadcasted_iota(jnp.int32, (n, wide), 0)
    w_col = lax.broadcasted_iota(jnp.int32, (n, wide), 1)
    own_seq = (w_row >> t_shift) == (w_col >> lane_shift)
    own_last = own_seq & ((w_row & (t - 1)) == t - 1)
    w_head = [(w_col & GLA_DK) == 0, (w_col & GLA_DK) != 0]
    srow_head1 = (lax.broadcasted_iota(jnp.int32, (wide, GLA_DV), 0) & GLA_DK) != 0
    tile = lambda a: jnp.concatenate([a] * nseq, axis=1)

    for p in range(npair):
        qk = slice(p * LANES, (p + 1) * LANES)
        q = q_ref[:, qk] * (GLA_DK ** -0.5)
        k = k_ref[:, qk]
        b = _seg_cumsum(la_ref[:, qk], rg, t)
        r_first = _seg_first(b, rg, t)
        r_last = _seg_last(b, rg, t)
        q_sub = q * jnp.exp(b - r_first)
        k_sub = k * jnp.exp(r_first - b)
        q_state = q * jnp.exp(b)
        k_upd = k * jnp.exp(r_last - b)
        s_all = s0_ref[:, 2 * p:2 * p + 2].reshape(wide, GLA_DV)
        q_wide = jnp.where(own_seq, tile(q_state), 0.0)
        k_wide_t = jnp.where(own_seq, tile(k_upd), 0.0).T
        decay = jnp.exp(jnp.sum(jnp.where(own_last, tile(b), 0.0).T, axis=1, keepdims=True))
        upd = []
        for h in range(2):
            attn = jnp.where(m_seq, _dot_nt(jnp.where(head_lane[h], q_sub, 0.0), k_sub), 0.0)
            vcol = slice((2 * p + h) * GLA_DV, (2 * p + h + 1) * GLA_DV)
            vh = v_ref[:, vcol]
            o = _dot(jnp.where(w_head[h], q_wide, 0.0), s_all) + _dot(attn, vh)
            o_ref[:, vcol] = _gla_out(o, gate_ref[:, vcol], gn).astype(BF16)
            upd.append(_dot(k_wide_t, vh))
        s_new = decay * s_all + jnp.where(srow_head1, upd[1], upd[0])
        sn_ref[:, 2 * p:2 * p + 2] = s_new.reshape(nseq, 2, GLA_DK, GLA_DV)


def gla_sample(z, loga, gla_norm, cols, s0_all, layer, a, row0, bd, t, *, name):
    q0, k0, v0, g0 = cols
    nseq = 16 if bd % 16 == 0 else bd
    n = nseq * t
    assert n % 8 == 0 and row0 % n == 0 and t & (t - 1) == 0
    qk_w, v_w = GLA_HEADS * GLA_DK, GLA_HEADS * GLA_DV
    off = row0 // n
    return pl.pallas_call(
        functools.partial(_gla_sample_kernel, t=t, nseq=nseq),
        grid=(bd // nseq,),
        in_specs=[
            pl.BlockSpec((n, qk_w), lambda i: (i + off, q0 // qk_w)),
            pl.BlockSpec((n, qk_w), lambda i: (i + off, k0 // qk_w)),
            pl.BlockSpec((n, v_w), lambda i: (i + off, v0 // v_w)),
            pl.BlockSpec((n, v_w), lambda i: (i + off, g0 // v_w)),
            pl.BlockSpec((n, qk_w), lambda i: (i, 0)),
            pl.BlockSpec((1, GLA_DV), lambda i: (0, 0)),
            pl.BlockSpec((None, nseq, GLA_HEADS, GLA_DK, GLA_DV), lambda i: (layer, i, 0, 0, 0)),
            _ANY,
        ],
        out_specs=[
            pl.BlockSpec((n, v_w), lambda i: (i + off, 0)),
            pl.BlockSpec((nseq, GLA_HEADS, GLA_DK, GLA_DV), lambda i: (i, 0, 0, 0)),
        ],
        out_shape=[
            jax.ShapeDtypeStruct(a.shape, a.dtype),
            jax.ShapeDtypeStruct((bd, GLA_HEADS, GLA_DK, GLA_DV), F32),
        ],
        input_output_aliases={7: 0},
        compiler_params=_cparams(("parallel",)),
        name=name,
    )(z, z, z, z, loga, gla_norm.reshape(1, GLA_DV), s0_all, a)


def _mla_self_kernel(q_ref, c_ref, kpe_ref, w_ref, a_hbm, o_ref, k_scr, v_scr, *, tq, nq):
    del a_hbm
    i = pl.program_id(2)

    @pl.when(i == 0)
    def _():
        kv = _dot(c_ref[...].astype(BF16), w_ref[0])
        kpe = kpe_ref[...].astype(F32)
        for h in range(2):
            k_scr[h] = (kv[:, h * HEAD_PAD:(h + 1) * HEAD_PAD] + kpe).astype(BF16)
            v_scr[h] = kv[:, (2 + h) * HEAD_PAD:(3 + h) * HEAD_PAD].astype(BF16)

    t_i = lax.broadcasted_iota(jnp.int32, (tq, tq), 0)
    s_i = lax.broadcasted_iota(jnp.int32, (tq, tq), 1)
    causal = s_i <= t_i
    qs = [q_ref[:, h * HEAD_PAD:(h + 1) * HEAD_PAD] for h in range(2)]

    def step(kb, carry, masked):
        r0 = kb * tq
        new = []
        for h in range(2):
            m, l, acc = carry[h]
            s = _dot_nt(qs[h], k_scr[h, r0:r0 + tq, :])
            if masked:
                s = jnp.where(causal, s, -jnp.inf)
            m_new = jnp.maximum(m, jnp.max(s, axis=1, keepdims=True))
            alpha = jnp.exp2(m - m_new)
            p = jnp.exp2(s - m_new)
            l = alpha * l + jnp.sum(p, axis=1, keepdims=True)
            acc = alpha * acc + _dot(p.astype(BF16), v_scr[h, r0:r0 + tq, :])
            new.append((m_new, l, acc))
        return tuple(new)

    init1 = (jnp.full((tq, 1), -jnp.inf, F32), jnp.zeros((tq, 1), F32), jnp.zeros((tq, HEAD_PAD), F32))

    def run(nfull):
        carry = (init1, init1)
        for kb in range(nfull):
            carry = step(kb, carry, False)
        (_, l0, acc0), (_, l1, acc1) = step(nfull, carry, True)
        o_ref[...] = (acc0 / l0 + acc1 / l1).astype(BF16)

    for iq in range(nq):
        pl.when(i == iq)(functools.partial(run, iq))


def mla_self(qp, c_new, kpe, wkv_pair, a, col0, bp, s, *, name):
    tq = _row_tile(s, 512)
    nq = s // tq
    npair = MLA_HEADS // 2
    kvl = c_new.shape[1]
    ow = 2 * MLA_V
    assert col0 % ow == 0
    return pl.pallas_call(
        functools.partial(_mla_self_kernel, tq=tq, nq=nq),
        grid=(bp, npair, nq),
        in_specs=[
            pl.BlockSpec((tq, 2 * HEAD_PAD), lambda b, p, i: (b * nq + i, p)),
            pl.BlockSpec((s, kvl), lambda b, p, i: (b, 0)),
            pl.BlockSpec((s, LANES), lambda b, p, i: (b, 0)),
            pl.BlockSpec((1, kvl, 4 * HEAD_PAD), lambda b, p, i: (p, 0, 0)),
            _ANY,
        ],
        out_specs=pl.BlockSpec((tq, ow), lambda b, p, i: (b * nq + i, col0 // ow + p)),
        out_shape=jax.ShapeDtypeStruct(a.shape, a.dtype),
        scratch_shapes=[pltpu.VMEM((2, s, HEAD_PAD), BF16), pltpu.VMEM((2, s, HEAD_PAD), BF16)],
        input_output_aliases={4: 0},
        compiler_params=_cparams(("parallel", "parallel", "arbitrary")),
        name=name,
    )(qp, c_new, kpe, wkv_pair, a)


def _q_absorb_kernel(q_ref, w_ref, o_ref):
    o_ref[0] = _dot(q_ref[...], w_ref[0])


def q_absorb(qp, wq_abs, *, name):
    rows = qp.shape[0]
    wout = wq_abs.shape[2]
    return pl.pallas_call(
        _q_absorb_kernel,
        grid=(MLA_HEADS,),
        in_specs=[
            pl.BlockSpec((rows, HEAD_PAD), lambda h: (0, h)),
            pl.BlockSpec((1, HEAD_PAD, wout), lambda h: (h, 0, 0)),
        ],
        out_specs=pl.BlockSpec((1, rows, wout), lambda h: (h, 0, 0)),
        out_shape=jax.ShapeDtypeStruct((MLA_HEADS, rows, wout), F32),
        compiler_params=_cparams(("parallel",)),
        name=name,
    )(qp, wq_abs)


def _mla_cached_kernel(pt_ref, q_ref, cn_ref, krn_ref, ckv_hbm, ckr_hbm, o_ref, kv_buf, kr_buf, sem,
                       *, layer, npages, page, t, kvl, nchunk, group):
    b = pl.program_id(0)
    nb = pl.num_programs(0)
    slot = b % 2
    cw = npages // nchunk * page

    def copies(step, sl, u, p):
        pg = pt_ref[step * group + u, p]
        rows = pl.ds(p * page, page)
        return (pltpu.make_async_copy(ckv_hbm.at[layer, pg], kv_buf.at[sl, u, rows], sem.at[0, sl]),
                pltpu.make_async_copy(ckr_hbm.at[layer, pg], kr_buf.at[sl, u, :, rows], sem.at[1, sl]))

    def start_all(step, sl):
        for u in range(group):
            for p in range(npages):
                for cp in copies(step, sl, u, p):
                    cp.start()

    @pl.when(b == 0)
    def _():
        start_all(0, 0)

    @pl.when(b + 1 < nb)
    def _():
        start_all(b + 1, 1 - slot)

    n = q_ref.shape[1]
    tok = lax.broadcasted_iota(jnp.int32, (n, 1), 0) & (t - 1)
    state = []
    for u in range(group):
        q = q_ref[u]
        q_lat = q[:, :kvl]
        q_pe = q[:, kvl:kvl + MLA_ROPE]
        c_new = cn_ref[u]
        kr_new = krn_ref[u]
        s_new = []
        for jn in range(t):
            sj = (jnp.sum(q_lat * c_new[jn:jn + 1, :], axis=1, keepdims=True)
                  + jnp.sum(q_pe * kr_new[jn:jn + 1, :], axis=1, keepdims=True))
            s_new.append(jnp.where(tok >= jn, sj, -jnp.inf))
        m = s_new[0]
        for sj in s_new[1:]:
            m = jnp.maximum(m, sj)
        l = jnp.zeros((n, 1), F32)
        o = jnp.zeros((n, kvl), F32)
        for sj, jn in zip(s_new, range(t)):
            pj = jnp.exp2(sj - m)
            l = l + pj
            o = o + pj * c_new[jn:jn + 1, :]
        q_rows = jnp.concatenate([q_lat.astype(BF16), jnp.zeros((LANES - n, kvl), BF16)], axis=0)
        state.append((m, l, o, q_rows, q_pe))
    for u in range(group):
        for p in range(npages):
            for cp in copies(b, slot, u, p):
                cp.wait()
    for ch in range(nchunk):
        for u in range(group):
            m, l, o, q_rows, q_pe = state[u]
            past_c = kv_buf[slot, u, ch * cw:(ch + 1) * cw, :].astype(BF16)
            past_rt = kr_buf[slot, u, :, ch * cw:(ch + 1) * cw]
            s = _dot_nt(past_c, q_rows).T[:n] + _dot(q_pe, past_rt)
            m_new = jnp.maximum(m, jnp.max(s, axis=1, keepdims=True))
            alpha = jnp.exp2(m - m_new)
            p_c = jnp.exp2(s - m_new)
            l = alpha * l + jnp.sum(p_c, axis=1, keepdims=True)
            o = alpha * o + _dot(p_c.astype(BF16), past_c)
            state[u] = (m_new, l, o, q_rows, q_pe)
    for u in range(group):
        _, l, o, _, _ = state[u]
        o_ref[u] = o / l


def mla_cached(page_table, q_abs, c_new_s, kr_new_s, cache_kv, cache_kr_t, layer, *, name):
    bd, npages = page_table.shape
    page, kvl = cache_kv.shape[2], cache_kv.shape[3]
    rope = cache_kr_t.shape[2]
    t = c_new_s.shape[1]
    n, qw = q_abs.shape[1], q_abs.shape[2]
    past = npages * page
    nchunk = 4 if npages % 4 == 0 else 1
    group = 1
    assert t & (t - 1) == 0 and n <= LANES
    grid_spec = pltpu.PrefetchScalarGridSpec(
        num_scalar_prefetch=1,
        grid=(bd // group,),
        in_specs=[
            pl.BlockSpec((group, n, qw), lambda b, pt: (b, 0, 0)),
            pl.BlockSpec((group, t, kvl), lambda b, pt: (b, 0, 0)),
            pl.BlockSpec((group, t, rope), lambda b, pt: (b, 0, 0)),
            _ANY,
            _ANY,
        ],
        out_specs=pl.BlockSpec((group, n, kvl), lambda b, pt: (b, 0, 0)),
        scratch_shapes=[
            pltpu.VMEM((2, group, past, kvl), F32),
            pltpu.VMEM((2, group, rope, past), F32),
            pltpu.SemaphoreType.DMA((2, 2)),
        ],
    )
    return pl.pallas_call(
        functools.partial(_mla_cached_kernel, layer=layer, npages=npages, page=page, t=t, kvl=kvl,
                          nchunk=nchunk, group=group),
        grid_spec=grid_spec,
        out_shape=jax.ShapeDtypeStruct((bd, n, kvl), F32),
        compiler_params=_cparams(("arbitrary",)),
        name=name,
    )(page_table, q_abs, c_new_s, kr_new_s, cache_kv, cache_kr_t)


def _v_up_kernel(o_ref, w_ref, a_hbm, out_ref):
    del a_hbm
    out_ref[...] = (_dot(o_ref[0].astype(BF16), w_ref[0, 0]) + _dot(o_ref[1].astype(BF16), w_ref[0, 1])).astype(BF16)


def v_up(o_lat, wv_pair, a, row0, col0, *, name):
    h, rows, kvl = o_lat.shape
    ow = 2 * MLA_V
    assert row0 % rows == 0 and col0 % ow == 0
    return pl.pallas_call(
        _v_up_kernel,
        grid=(h // 2,),
        in_specs=[
            pl.BlockSpec((2, rows, kvl), lambda p: (p, 0, 0)),
            pl.BlockSpec((1, 2, kvl, ow), lambda p: (p, 0, 0, 0)),
            _ANY,
        ],
        out_specs=pl.BlockSpec((rows, ow), lambda p: (row0 // rows, col0 // ow + p)),
        out_shape=jax.ShapeDtypeStruct(a.shape, a.dtype),
        input_output_aliases={2: 0},
        compiler_params=_cparams(("parallel",)),
        name=name,
    )(o_lat, wv_pair, a)


CONV_PAD = 32
CONV_SUB = 64
CONV_LANES = 256


def _ln_swish(y, lg, lb):
    mu = jnp.mean(y, axis=-1, keepdims=True)
    yc = y - mu
    var = jnp.mean(yc * yc, axis=-1, keepdims=True)
    yn = yc * lax.rsqrt(var + EPS) * lg + lb
    return yn * _sigmoid(yn)


def _conv_prompt_kernel(u_ref, halo_ref, x_ref, wdw_ref, bdw_ref, lg_ref, lb_ref, w2_ref, b2_ref,
                        o_ref, st_ref, ext_ref, sh_ref, h_ref, *, tt, cw):
    hist = cw - 1
    lo = CONV_PAD - hist
    d = u_ref.shape[1]
    j = pl.program_id(1)

    @pl.when(j == 0)
    def _():
        ext_ref[0:CONV_PAD, :] = jnp.zeros((CONV_PAD, d), F32)

    @pl.when(j > 0)
    def _():
        ext_ref[0:CONV_PAD, :] = halo_ref[...]

    ext_ref[CONV_PAD:CONV_PAD + tt, :] = u_ref[...]
    st_ref[0] = ext_ref[CONV_PAD + tt - hist:CONV_PAD + tt, :]
    span = tt + CONV_PAD - SUBLANES
    for r in range(1, SUBLANES):
        for c0 in range(0, d, CONV_LANES):
            sh_ref[r - 1, 0:span, c0:c0 + CONV_LANES] = ext_ref[r:r + span, c0:c0 + CONV_LANES]

    grp = (CONV_SUB // SUBLANES, SUBLANES, LANES)
    for c0 in range(0, d, LANES):
        cs = slice(c0, c0 + LANES)
        wts = [wdw_ref[w, :, cs] for w in range(cw)]
        bias = jnp.broadcast_to(bdw_ref[:, cs].reshape(1, 1, LANES), grp)

        for r0 in range(0, tt, CONV_SUB):
            acc = bias
            for w in range(cw):
                r, a = (w + lo) % SUBLANES, (w + lo) // SUBLANES
                rows = slice(r0 + a * SUBLANES, r0 + a * SUBLANES + CONV_SUB)
                src = ext_ref[rows, cs] if r == 0 else sh_ref[r - 1, rows, cs]
                acc = acc + src.reshape(grp) * wts[w][None]
            h_ref[r0:r0 + CONV_SUB, cs] = acc.reshape(CONV_SUB, LANES)

    for r0 in range(0, tt, CONV_SUB):
        h_ref[r0:r0 + CONV_SUB, :] = _ln_swish(h_ref[r0:r0 + CONV_SUB, :], lg_ref[...], lb_ref[...])
    o_ref[...] = x_ref[...] + b2_ref[...] + _dot(h_ref[...].astype(BF16), w2_ref[...])


def conv_prompt(u, x, w_dw, b_dw, ln_g, ln_b, w2_bf16, layer, b2, bp, s, *, name):
    r, d = x.shape
    cw = w_dw.shape[0]
    hist = cw - 1
    tt = _row_tile(s, 512)
    assert hist <= CONV_PAD and tt % CONV_SUB == 0 and tt % CONV_PAD == 0 and d % CONV_LANES == 0
    nj = s // tt
    hb = tt // CONV_PAD
    blk = pl.BlockSpec((tt, d), lambda b, j: (b * nj + j, 0))
    return pl.pallas_call(
        functools.partial(_conv_prompt_kernel, tt=tt, cw=cw),
        grid=(bp, nj),
        in_specs=[
            blk,
            pl.BlockSpec((CONV_PAD, d), lambda b, j: (jnp.maximum((b * nj + j) * hb - 1, 0), 0)),
            blk,
            _const_spec((cw, SUBLANES, d)), _const_spec((1, d)), _const_spec((1, d)), _const_spec((1, d)),
            _layer_spec((d, d), layer), _const_spec((1, d)),
        ],
        out_specs=[blk, pl.BlockSpec((1, hist, d), lambda b, j: (b, 0, 0))],
        out_shape=[jax.ShapeDtypeStruct((r, d), F32), jax.ShapeDtypeStruct((bp, hist, d), F32)],
        scratch_shapes=[
            pltpu.VMEM((tt + CONV_PAD, d), F32),
            pltpu.VMEM((SUBLANES - 1, tt + CONV_PAD, d), F32),
            pltpu.VMEM((tt, d), F32),
        ],
        input_output_aliases={2: 0},
        compiler_params=_cparams(("parallel", "arbitrary")),
        name=name,
    )(u, u, x, jnp.broadcast_to(w_dw[:, None, :], (cw, SUBLANES, d)), b_dw.reshape(1, d), ln_g.reshape(1, d),
      ln_b.reshape(1, d), w2_bf16, b2.reshape(1, d))


def _conv_sample_kernel(u_ref, buf_ref, x_ref, wdw_ref, bdw_ref, lg_ref, lb_ref, w2_ref, b2_ref,
                        o_ref, st_ref, ext_ref, h_ref, *, t, bb, cw):
    hist = cw - 1
    lo = CONV_PAD - hist
    for i in range(bb):
        ext_ref[i, lo:CONV_PAD, :] = buf_ref[i]
        ext_ref[i, CONV_PAD:CONV_PAD + t, :] = u_ref[i * t:(i + 1) * t, :]
        st_ref[i] = ext_ref[i, CONV_PAD + t - hist:CONV_PAD + t, :]
        acc = jnp.broadcast_to(bdw_ref[...], (t, bdw_ref.shape[1]))
        for w in range(cw):
            acc = acc + ext_ref[i, lo + w:lo + w + t, :] * wdw_ref[w:w + 1, :]
        h_ref[i * t:(i + 1) * t, :] = _ln_swish(acc, lg_ref[...], lb_ref[...])
    o_ref[...] = x_ref[...] + b2_ref[...] + _dot(h_ref[...].astype(BF16), w2_ref[...])


def conv_sample(u, buf, x, w_dw, b_dw, ln_g, ln_b, w2_bf16, layer, b2, row0, bd, t, *, name):
    r, d = x.shape
    cw = w_dw.shape[0]
    hist = cw - 1
    bb = 32 if bd % 32 == 0 else bd
    rows = bb * t
    assert hist <= CONV_PAD and rows % 8 == 0 and row0 % rows == 0
    off = row0 // rows
    blk = pl.BlockSpec((rows, d), lambda i: (i + off, 0))
    ext_rows = -(-(CONV_PAD + t) // SUBLANES) * SUBLANES
    return pl.pallas_call(
        functools.partial(_conv_sample_kernel, t=t, bb=bb, cw=cw),
        grid=(bd // bb,),
        in_specs=[
            blk,
            pl.BlockSpec((bb, hist, d), lambda i: (i, 0, 0)),
            blk,
            _const_spec((cw, d)), _const_spec((1, d)), _const_spec((1, d)), _const_spec((1, d)),
            _layer_spec((d, d), layer), _const_spec((1, d)),
        ],
        out_specs=[blk, pl.BlockSpec((bb, hist, d), lambda i: (i, 0, 0))],
        out_shape=[jax.ShapeDtypeStruct((r, d), F32), jax.ShapeDtypeStruct((bd, hist, d), F32)],
        scratch_shapes=[pltpu.VMEM((bb, ext_rows, d), F32), pltpu.VMEM((rows, d), F32)],
        input_output_aliases={2: 0},
        compiler_params=_cparams(("parallel",)),
        name=name,
    )(u, buf, x, w_dw, b_dw.reshape(1, d), ln_g.reshape(1, d), ln_b.reshape(1, d), w2_bf16, b2.reshape(1, d))


def _rope_tables(pos):
    inv = jnp.power(ROPE_THETA, -jnp.arange(0, MLA_ROPE, 2, dtype=F32) / MLA_ROPE)
    ang = pos.astype(F32)[:, None] * inv[None, :]
    cos, sin = jnp.cos(ang), jnp.sin(ang)
    n = pos.shape[0]
    lo = ROPE_LO
    cos_t = jnp.ones((n, LANES), F32).at[:, lo:lo + MLA_ROPE].set(jnp.concatenate([cos, cos], axis=1))
    sin_a = jnp.zeros((n, LANES), F32).at[:, lo:lo + ROPE_HALF].set(-sin)
    sin_b = jnp.zeros((n, LANES), F32).at[:, lo + ROPE_HALF:lo + MLA_ROPE].set(sin)
    return cos_t, sin_a, sin_b


def _layout_w_in(w_in, gla_qk, gla_v, q_lora, kv_lora):
    nl, d = w_in.shape[0], w_in.shape[1]
    sp = [0, gla_qk, 2 * gla_qk, 2 * gla_qk + gla_v, 2 * gla_qk + gla_v + GLA_RANK]
    sp += [sp[-1] + gla_v, sp[-1] + gla_v + q_lora, sp[-1] + gla_v + q_lora + kv_lora]
    q, k, v, a, g, dq, dkv, kr = [w_in[..., lo:hi] for lo, hi in zip(sp, sp[1:] + [w_in.shape[2]])]
    last = jnp.zeros((nl, d, LANES), w_in.dtype).at[..., :GLA_RANK].set(a)
    last = last.at[..., ROPE_LO:ROPE_LO + MLA_ROPE].set(kr)
    w = jnp.concatenate([q, k, v, g, dq, dkv, last], axis=2)
    cols = dict(q=0, k=gla_qk, v=2 * gla_qk, g=2 * gla_qk + gla_v, dq=2 * gla_qk + 2 * gla_v)
    cols["dkv"] = cols["dq"] + q_lora
    cols["akr"] = cols["dkv"] + kv_lora
    return w.astype(BF16), cols


def _layout_w_uq(w_uq):
    ql = w_uq.shape[0]
    w = w_uq.reshape(ql, MLA_HEADS, MLA_NOPE + MLA_ROPE)
    w = jnp.pad(w, ((0, 0), (0, 0), (0, HEAD_PAD - MLA_NOPE - MLA_ROPE)))
    return w.reshape(ql, MLA_HEADS * HEAD_PAD).astype(BF16)


def _layout_w_ukv(w_ukv):
    kvl = w_ukv.shape[0]
    w = w_ukv.reshape(kvl, MLA_HEADS, MLA_NOPE + MLA_V)
    wk = jnp.pad(w[..., :MLA_NOPE], ((0, 0), (0, 0), (0, HEAD_PAD - MLA_NOPE)))
    wv = w[..., MLA_NOPE:]
    wv_e = jnp.pad(wv[:, 0::2], ((0, 0), (0, 0), (0, MLA_V)))
    wv_o = jnp.pad(wv[:, 1::2], ((0, 0), (0, 0), (MLA_V, 0)))
    pair = jnp.stack([wk[:, 0::2], wk[:, 1::2], wv_e, wv_o], axis=2)
    pair = pair.transpose(1, 0, 2, 3).reshape(MLA_HEADS // 2, kvl, 4 * HEAD_PAD)
    w_uk_t = w[..., :MLA_NOPE].transpose(1, 2, 0)
    wq_abs = jnp.zeros((MLA_HEADS, HEAD_PAD, kvl + LANES), F32)
    wq_abs = wq_abs.at[:, :MLA_NOPE, :kvl].set(w_uk_t)
    wq_abs = wq_abs.at[:, ROPE_LO:ROPE_LO + MLA_ROPE, kvl:kvl + MLA_ROPE].set(jnp.eye(MLA_ROPE, dtype=F32))
    wv_h = wv.transpose(1, 0, 2)
    wv_pair = jnp.stack([jnp.pad(wv_h[0::2], ((0, 0), (0, 0), (0, MLA_V))),
                         jnp.pad(wv_h[1::2], ((0, 0), (0, 0), (MLA_V, 0)))], axis=1)
    return pair.astype(BF16), wq_abs.astype(BF16), wv_pair.astype(BF16)


def kernel(x_prompt, x_sample, cache_kv, cache_kr, state_gla, state_conv, page_table, norm_mix, norm_mlp,
           norm_final, w_in, w_gate_a2, b_gate_a, gla_norm, mla_q_norm, mla_kv_norm, w_uq, w_ukv, w_out_ab,
           w_pw1, b_pw1, w_dw, b_dw, conv_ln_g, conv_ln_b, w_pw2, b_pw2, w_up, w_down):
    bp, s, d = x_prompt.shape
    bd, t, _ = x_sample.shape
    rp, rs = bp * s, bd * t
    r = rp + rs
    depth = norm_mix.shape[0]
    page = cache_kv.shape[2]
    past = page_table.shape[1] * page
    gla_qk, gla_v = GLA_HEADS * GLA_DK, GLA_HEADS * GLA_DV
    mla_w = MLA_HEADS * MLA_V
    q_lora, kv_lora = mla_q_norm.shape[1], mla_kv_norm.shape[1]
    d_ff = w_up.shape[2]
    tf = 512 if d_ff % 512 == 0 else d_ff

    tabs_p = _rope_tables(jnp.arange(s))
    tabs_s = _rope_tables(jnp.tile(past + jnp.arange(t), bd))
    cache_kr_t = jnp.swapaxes(cache_kr, 2, 3)

    w_in_all, cols = _layout_w_in(w_in, gla_qk, gla_v, q_lora, kv_lora)
    zero_bias = jnp.zeros((w_in_all.shape[2],), F32)
    w_pw1_all, w_pw2_all, w_out_all = w_pw1.astype(BF16), w_pw2.astype(BF16), w_out_ab.astype(BF16)
    w_up_all = w_up.astype(BF16)
    w_down_all = w_down.astype(BF16).reshape(depth, d_ff // tf, tf, d)

    x = jnp.concatenate([x_prompt.reshape(rp, d), x_sample.reshape(rs, d)], axis=0)
    kv_p, kr_p, gla_p, conv_p, kv_s, kr_s, gla_s, conv_s = ([] for _ in range(8))
    for l in range(depth):
        i = l // 2
        if l % 2 == 0:
            z = norm_matmul(x, norm_mix[l], w_in_all, i, zero_bias, name=f"in_proj{l}")
            w2_pad = jnp.zeros((LANES, gla_qk), F32).at[:GLA_RANK].set(w_gate_a2[i]).astype(BF16)
            pcols = (cols["dq"], cols["dkv"], cols["akr"])
            pargs = (mla_q_norm[i], mla_kv_norm[i], _layout_w_uq(w_uq[i]), w2_pad, b_gate_a[i])
            loga_p, qp_p, c_p, krn_p, kpe_p = ab_proj(z, pcols, tabs_p, 0, rp, *pargs, name=f"ab_proj_prompt{l}")
            loga_s, qp_s, c_s, krn_s, _ = ab_proj(z, pcols, tabs_s, rp, rs, *pargs, name=f"ab_proj_sample{l}")
            gcols = (cols["q"], cols["k"], cols["v"], cols["g"])
            a = jnp.zeros((r, gla_v + mla_w), BF16)
            a, sg_p = gla_prompt(z, loga_p, gla_norm[i], gcols, a, bp, s, name=f"gla_prompt{l}")
            a, sg_s = gla_sample(z, loga_s, gla_norm[i], gcols, state_gla, i, a, rp, bd, t, name=f"gla_sample{l}")
            wkv_pair, wq_abs, wv_pair = _layout_w_ukv(w_ukv[i])
            a = mla_self(qp_p, c_p, kpe_p, wkv_pair, a, gla_v, bp, s, name=f"mla_self{l}")
            q_abs = q_absorb(qp_s, wq_abs, name=f"q_absorb{l}")
            q_abs = q_abs.reshape(MLA_HEADS, bd, t, -1).transpose(1, 0, 2, 3).reshape(bd, MLA_HEADS * t, -1)
            c_s3 = c_s.reshape(bd, t, kv_lora)
            krn_s3 = krn_s.reshape(bd, t, MLA_ROPE)
            o_lat = mla_cached(page_table, q_abs, c_s3, krn_s3, cache_kv, cache_kr_t, i, name=f"mla_cached{l}")
            o_lat = o_lat.reshape(bd, MLA_HEADS, t, kv_lora).transpose(1, 0, 2, 3).reshape(MLA_HEADS, rs, kv_lora)
            a = v_up(o_lat, wv_pair, a, rp, gla_v, name=f"v_up{l}")
            x = out_proj_residual(a, w_out_all, i, x, name=f"out_proj{l}")
            kv_p.append(c_p.reshape(bp, s, kv_lora)); kr_p.append(krn_p.reshape(bp, s, MLA_ROPE))
            kv_s.append(c_s3); kr_s.append(krn_s3)
            gla_p.append(sg_p); gla_s.append(sg_s)
        else:
            u = norm_matmul(x, norm_mix[l], w_pw1_all, i, b_pw1[i], glu=True, name=f"pw1_glu{l}")
            cargs = (w_dw[i], b_dw[i], conv_ln_g[i], conv_ln_b[i], w_pw2_all, i, b_pw2[i])
            x, st_p = conv_prompt(u, x, *cargs, bp, s, name=f"conv_prompt{l}")
            x, st_s = conv_sample(u, state_conv[i], x, *cargs, rp, bd, t, name=f"conv_sample{l}")
            conv_p.append(st_p); conv_s.append(st_s)
        x = mlp_residual(x, norm_mlp[l], w_up_all, w_down_all, l, name=f"mlp{l}")
    y_p = final_norm(x, norm_final, 0, rp, name="final_norm_prompt").reshape(bp, s, d)
    y_s = final_norm(x, norm_final, rp, rs, name="final_norm_sample").reshape(bd, t, d)
    return (y_p, y_s, jnp.stack(kv_p), jnp.stack(kr_p), jnp.stack(gla_p), jnp.stack(conv_p),
            jnp.stack(kv_s), jnp.stack(kr_s), jnp.stack(gla_s), jnp.stack(conv_s))
```

```python
import functools
import math

import jax
import jax.numpy as jnp
from jax import lax
from jax.experimental import pallas as pl
from jax.experimental.pallas import tpu as pltpu

F32 = jnp.float32
BF16 = jnp.bfloat16

GLA_HEADS = 4
GLA_DK = 64
GLA_DV = 128
GLA_RANK = 16
GLA_TAU = 16.0
GLA_CHUNK = 64
GLA_SUB = 16
MLA_HEADS = 8
MLA_NOPE = 64
MLA_ROPE = 32
MLA_V = 64
MLA_SCALE = (MLA_NOPE + MLA_ROPE) ** -0.5
Q_PRESCALE = MLA_SCALE * math.log2(math.e)
ROPE_THETA = 10000.0
EPS = 1e-6

LANES = 128
SUBLANES = 8
HEAD_PAD = 128
ROPE_LO = MLA_NOPE
ROPE_HALF = MLA_ROPE // 2
VMEM_LIMIT = 56 * 1024 * 1024


def _cparams(sem):
    return pltpu.CompilerParams(dimension_semantics=sem, vmem_limit_bytes=VMEM_LIMIT)


def _row_tile(n, target):
    best = None
    for t in range(8, min(n, target) + 1, 8):
        if n % t == 0:
            best = t
    assert best is not None, n
    return best


def _const_spec(shape):
    nd = len(shape)
    return pl.BlockSpec(shape, lambda *_: (0,) * nd, pipeline_mode=pl.Buffered(1))


def _layer_spec(shape, layer):
    nd = len(shape)
    return pl.BlockSpec((None,) + tuple(shape), lambda *_: (layer,) + (0,) * nd, pipeline_mode=pl.Buffered(1))


_ANY = pl.BlockSpec(memory_space=pl.ANY)


def _rms(x, g):
    ms = jnp.mean(x * x, axis=-1, keepdims=True)
    return x * lax.rsqrt(ms + EPS) * g


def _sigmoid(x):
    return 1.0 / (1.0 + jnp.exp(-x))


def _dot(a, b):
    return jnp.dot(a, b, preferred_element_type=F32)


def _dot_nt(a, b):
    return lax.dot_general(a, b, (((1,), (1,)), ((), ())), preferred_element_type=F32)


def _split_row_specs(tm, d, n_first):
    return [pl.BlockSpec((tm, d), lambda i: (jnp.minimum(i, n_first - 1), 0)),
            pl.BlockSpec((tm, d), lambda i: (jnp.maximum(i - n_first, 0), 0))]


def _split_row_tile(first_ref, second_ref, n_first):
    return jnp.where(pl.program_id(0) < n_first, first_ref[...], second_ref[...])


def _row_sources(x, target):
    if isinstance(x, tuple):
        (r0, d), r1 = x[0].shape, x[1].shape[0]
        tm = _row_tile(math.gcd(r0, r1), target)
        return x, r0 + r1, d, tm, r0 // tm
    return (x,), x.shape[0], x.shape[1], _row_tile(x.shape[0], target), None


def _norm_matmul_kernel(*refs, glu, n_first):
    if n_first is None:
        x_ref, g_ref, w_ref, b_ref, o_ref = refs
        x = x_ref[...]
    else:
        xa_ref, xb_ref, g_ref, w_ref, b_ref, o_ref = refs
        x = _split_row_tile(xa_ref, xb_ref, n_first)
    h = _rms(x, g_ref[...]).astype(BF16)
    u = _dot(h, w_ref[...]) + b_ref[...]
    if glu:
        half = u.shape[1] // 2
        u = u[:, :half] * _sigmoid(u[:, half:])
    o_ref[...] = u


def norm_matmul(x, g, w_bf16, layer, bias, *, glu=False, name):
    xs, r, d, tm, n_first = _row_sources(x, 512)
    n = w_bf16.shape[2]
    n_out = n // 2 if glu else n
    x_specs = [pl.BlockSpec((tm, d), lambda i: (i, 0))] if n_first is None else _split_row_specs(tm, d, n_first)
    return pl.pallas_call(
        functools.partial(_norm_matmul_kernel, glu=glu, n_first=n_first),
        grid=(r // tm,),
        in_specs=x_specs + [
            _const_spec((1, d)),
            _layer_spec((d, n), layer),
            _const_spec((1, n)),
        ],
        out_specs=pl.BlockSpec((tm, n_out), lambda i: (i, 0)),
        out_shape=jax.ShapeDtypeStruct((r, n_out), F32),
        compiler_params=_cparams(("parallel",)),
        name=name,
    )(*xs, g.reshape(1, d), w_bf16, bias.reshape(1, n))


def _mlp_kernel(x_ref, g_ref, wup_ref, wdn_ref, *rest, final):
    if final:
        fg_ref, o_ref, h_ref = rest
    else:
        o_ref, h_ref = rest
    x = x_ref[...]
    h_ref[...] = _rms(x, g_ref[...]).astype(BF16)
    o_ref[...] = x

    nf, tf = wdn_ref.shape[0], wdn_ref.shape[1]

    def body(f, carry):
        c0 = pl.multiple_of(f * tf, tf)
        u = jnp.maximum(_dot(h_ref[...], wup_ref[:, pl.ds(c0, tf)]), 0.0)
        o_ref[...] += _dot((u * u).astype(BF16), wdn_ref[f])
        return carry

    lax.fori_loop(0, nf, body, 0, unroll=2 if nf % 2 == 0 else 1)
    if final:
        o_ref[...] = _rms(o_ref[...], fg_ref[...])


def mlp_residual(x, g, wup, wdn4, layer, row0=0, rows=None, final_g=None, *, name):
    d = x.shape[1]
    rows = x.shape[0] if rows is None else rows
    _, nf, tf, _ = wdn4.shape
    tm = _row_tile(math.gcd(row0, rows) if row0 else rows, 1024)
    off = row0 // tm
    final = final_g is not None
    extra_specs, extra = ([_const_spec((1, d))], [final_g.reshape(1, d)]) if final else ([], [])
    return pl.pallas_call(
        functools.partial(_mlp_kernel, final=final),
        grid=(rows // tm,),
        in_specs=[
            pl.BlockSpec((tm, d), lambda i: (i + off, 0)),
            _const_spec((1, d)),
            _layer_spec((d, nf * tf), layer),
            _layer_spec((nf, tf, d), layer),
        ] + extra_specs,
        out_specs=pl.BlockSpec((tm, d), lambda i: (i, 0)),
        out_shape=jax.ShapeDtypeStruct((rows, d), F32),
        scratch_shapes=[pltpu.VMEM((tm, d), BF16)],
        compiler_params=_cparams(("parallel",)),
        name=name,
    )(x, g.reshape(1, d), wup, wdn4, *extra)


def _out_proj_kernel(a_ref, w_ref, *rest, n_first):
    if n_first is None:
        r_ref, o_ref = rest
        res = r_ref[...]
    else:
        ra_ref, rb_ref, o_ref = rest
        res = _split_row_tile(ra_ref, rb_ref, n_first)
    o_ref[...] = res + _dot(a_ref[...], w_ref[...])


def out_proj_residual(a_bf16, w_bf16, layer, res, *, name):
    k = a_bf16.shape[1]
    rs, r, d, tm, n_first = _row_sources(res, 512)
    r_specs = [pl.BlockSpec((tm, d), lambda i: (i, 0))] if n_first is None else _split_row_specs(tm, d, n_first)
    return pl.pallas_call(
        functools.partial(_out_proj_kernel, n_first=n_first),
        grid=(r // tm,),
        in_specs=[
            pl.BlockSpec((tm, k), lambda i: (i, 0)),
            _layer_spec((k, d), layer),
        ] + r_specs,
        out_specs=pl.BlockSpec((tm, d), lambda i: (i, 0)),
        out_shape=jax.ShapeDtypeStruct((r, d), F32),
        compiler_params=_cparams(("parallel",)),
        name=name,
    )(a_bf16, w_bf16, *rs)


def _rope_block(x, cos_t, sin_a, sin_b):
    up = pltpu.roll(x, LANES - ROPE_HALF, 1)
    dn = pltpu.roll(x, ROPE_HALF, 1)
    return x * cos_t + up * sin_a + dn * sin_b


def _ab_proj_kernel(dq_ref, dkv_ref, akr_ref, cos_ref, sa_ref, sb_ref, qn_ref, kvn_ref,
                    wuq_ref, w2_ref, b2_ref, loga_ref, q_ref, c_ref, kr_ref, kpe_ref):
    cos_t, sin_a, sin_b = cos_ref[...], sa_ref[...], sb_ref[...]
    cq = _rms(dq_ref[...], qn_ref[...]).astype(BF16)
    qm = _dot(cq, wuq_ref[...])
    for h in range(MLA_HEADS):
        blk = qm[:, h * HEAD_PAD:(h + 1) * HEAD_PAD]
        roped = _rope_block(blk, cos_t, sin_a, sin_b)
        q_ref[:, h * HEAD_PAD:(h + 1) * HEAD_PAD] = (roped * Q_PRESCALE).astype(BF16)
    c_ref[...] = _rms(dkv_ref[...], kvn_ref[...])
    akr = akr_ref[...]
    y = _rope_block(akr, cos_t, sin_a, sin_b)
    lane = lax.broadcasted_iota(jnp.int32, y.shape, 1)
    rope_lane = (lane >= ROPE_LO) & (lane < ROPE_LO + MLA_ROPE)
    kpe_ref[...] = jnp.where(rope_lane, y, 0.0).astype(BF16)
    kr_ref[...] = y[:, ROPE_LO:ROPE_LO + MLA_ROPE]
    xa = _dot(akr.astype(BF16), w2_ref[...]) + b2_ref[...]
    log_sig = jnp.minimum(xa, 0.0) - jnp.log(1.0 + jnp.exp(-jnp.abs(xa)))
    loga_ref[...] = log_sig * (1.0 / GLA_TAU)


def ab_proj(z, cols, tabs, row0, rows, q_norm, kv_norm, wuq_pad, w2_pad, b2, *, name):
    tab_rows = tabs[0].shape[0]
    tm = _row_tile(math.gcd(math.gcd(row0, rows) if row0 else rows, tab_rows), 512)
    off = row0 // tm
    ntab = tab_rows // tm
    dq0, dkv0, akr0 = cols
    ql, kvl = q_norm.shape[0], kv_norm.shape[0]
    nq = wuq_pad.shape[1]
    gw = w2_pad.shape[1]
    out_row = lambda w: pl.BlockSpec((tm, w), lambda i: (i, 0))
    tab = pl.BlockSpec((tm, LANES), lambda i: (i % ntab, 0))
    return pl.pallas_call(
        _ab_proj_kernel,
        grid=(rows // tm,),
        in_specs=[
            pl.BlockSpec((tm, ql), lambda i: (i + off, dq0 // ql)),
            pl.BlockSpec((tm, kvl), lambda i: (i + off, dkv0 // kvl)),
            pl.BlockSpec((tm, LANES), lambda i: (i + off, akr0 // LANES)),
            tab, tab, tab,
            _const_spec((1, ql)), _const_spec((1, kvl)),
            _const_spec((ql, nq)), _const_spec((LANES, gw)), _const_spec((1, gw)),
        ],
        out_specs=[out_row(gw), out_row(nq), out_row(kvl), out_row(MLA_ROPE), out_row(LANES)],
        out_shape=[
            jax.ShapeDtypeStruct((rows, gw), F32),
            jax.ShapeDtypeStruct((rows, nq), BF16),
            jax.ShapeDtypeStruct((rows, kvl), F32),
            jax.ShapeDtypeStruct((rows, MLA_ROPE), F32),
            jax.ShapeDtypeStruct((rows, LANES), BF16),
        ],
        compiler_params=_cparams(("parallel",)),
        name=name,
    )(z, z, z, *tabs, q_norm.reshape(1, ql), kv_norm.reshape(1, kvl), wuq_pad, w2_pad, b2.reshape(1, gw))


def _seg_cumsum(x, rg, group):
    k = 1
    while k < group:
        x = x + jnp.where(rg >= k, pltpu.roll(x, k, 0), 0.0)
        k *= 2
    return x


def _seg_first(x, rg, group):
    y = jnp.where(rg == 0, x, 0.0)
    k = 1
    while k < group:
        y = y + pltpu.roll(y, k, 0)
        k *= 2
    return y


def _seg_last(x, rg, group):
    n = x.shape[0]
    y = jnp.where(rg == group - 1, x, 0.0)
    k = 1
    while k < group:
        y = y + pltpu.roll(y, n - k, 0)
        k *= 2
    return y


def _gla_out(o, gate, gn):
    return _rms(o, gn) * (gate * _sigmoid(gate))


def _gla_prompt_kernel(q_ref, k_ref, v_ref, gate_ref, la_ref, gn_ref, a_hbm, o_ref, sf_ref, s_ref, *, nchunk):
    del a_hbm
    c = GLA_CHUNK
    npair = GLA_HEADS // 2
    j = pl.program_id(1)

    @pl.when(j == 0)
    def _():
        s_ref[...] = jnp.zeros_like(s_ref)

    gn = gn_ref[...]
    row = lax.broadcasted_iota(jnp.int32, (c, LANES), 0)
    lane = lax.broadcasted_iota(jnp.int32, (c, LANES), 1)
    head_lane = [lane < GLA_DK, lane >= GLA_DK]
    rg = row & (GLA_SUB - 1)
    t_i = lax.broadcasted_iota(jnp.int32, (c, c), 0)
    s_i = lax.broadcasted_iota(jnp.int32, (c, c), 1)
    sub_shift = GLA_SUB.bit_length() - 1
    tb, sb = t_i >> sub_shift, s_i >> sub_shift
    m_diag = (tb == sb) & (s_i <= t_i)
    m_next = (tb == sb + 1) & ((sb & 1) == 0)
    m_half = (t_i >= c // 2) & (s_i < c // 2)

    def chunk(ci, carry):
        r0 = pl.multiple_of(ci * c, c)
        for p in range(npair):
            qk = slice(p * LANES, (p + 1) * LANES)
            q = q_ref[pl.ds(r0, c), qk] * (GLA_DK ** -0.5)
            k = k_ref[pl.ds(r0, c), qk]
            b = _seg_cumsum(la_ref[pl.ds(r0, c), qk], row, c)
            r_sub = _seg_first(b, rg, GLA_SUB)
            r_next = pltpu.roll(r_sub, c - GLA_SUB, 0)
            b_half = b[c // 2:c // 2 + 1, :]
            q_sub = q * jnp.exp(b - r_sub)
            k_sub = k * jnp.exp(r_sub - b)
            k_next = k * jnp.exp(jnp.minimum(r_next - b, 0.0))
            q_half = q * jnp.exp(jnp.minimum(b - b_half, 0.0))
            k_half = k * jnp.exp(jnp.minimum(b_half - b, 0.0))
            q_state = q * jnp.exp(b)
            k_t = k.T
            b_t = b.T
            b_last = b_t[:, c - 1:c]
            k_upd = k_t * jnp.exp(b_last - b_t)
            s_decay = jnp.exp(b_last)
            s_pair = s_ref[p * LANES:(p + 1) * LANES, :]
            for h in range(2):
                hm = head_lane[h]
                z = lambda a: jnp.where(hm, a, 0.0)
                attn = (jnp.where(m_diag, _dot_nt(z(q_sub), k_sub), 0.0)
                        + jnp.where(m_next, _dot_nt(z(q_sub), k_next), 0.0)
                        + jnp.where(m_half, _dot_nt(z(q_half), k_half), 0.0))
                vcol = slice((2 * p + h) * GLA_DV, (2 * p + h + 1) * GLA_DV)
                vh = v_ref[pl.ds(r0, c), vcol]
                o = _dot(z(q_state), s_pair) + _dot(attn, vh)
                o_ref[pl.ds(r0, c), vcol] = _gla_out(o, gate_ref[pl.ds(r0, c), vcol], gn).astype(BF16)
                lo, hi = h * GLA_DK, (h + 1) * GLA_DK
                srow = slice(p * LANES + lo, p * LANES + hi)
                s_ref[srow, :] = s_decay[lo:hi] * s_pair[lo:hi] + _dot(k_upd[lo:hi], vh)
        return carry

    lax.fori_loop(0, nchunk, chunk, 0, unroll=4 if nchunk % 4 == 0 else 1)

    @pl.when(j == pl.num_programs(1) - 1)
    def _():
        for h in range(GLA_HEADS):
            sf_ref[0, h] = s_ref[h * GLA_DK:(h + 1) * GLA_DK, :]


def gla_prompt(z, loga, gla_norm, cols, a, bp, s, *, name):
    q0, k0, v0, g0 = cols
    tb = _row_tile(s, 512)
    assert tb % GLA_CHUNK == 0
    nj = s // tb
    qk_w, v_w = GLA_HEADS * GLA_DK, GLA_HEADS * GLA_DV
    rowblk = lambda b, j: b * nj + j
    return pl.pallas_call(
        functools.partial(_gla_prompt_kernel, nchunk=tb // GLA_CHUNK),
        grid=(bp, nj),
        in_specs=[
            pl.BlockSpec((tb, qk_w), lambda b, j: (rowblk(b, j), q0 // qk_w)),
            pl.BlockSpec((tb, qk_w), lambda b, j: (rowblk(b, j), k0 // qk_w)),
            pl.BlockSpec((tb, v_w), lambda b, j: (rowblk(b, j), v0 // v_w)),
            pl.BlockSpec((tb, v_w), lambda b, j: (rowblk(b, j), g0 // v_w)),
            pl.BlockSpec((tb, qk_w), lambda b, j: (rowblk(b, j), 0)),
            pl.BlockSpec((1, GLA_DV), lambda b, j: (0, 0)),
            _ANY,
        ],
        out_specs=[
            pl.BlockSpec((tb, v_w), lambda b, j: (rowblk(b, j), 0)),
            pl.BlockSpec((1, GLA_HEADS, GLA_DK, GLA_DV), lambda b, j: (b, 0, 0, 0)),
        ],
        out_shape=[
            jax.ShapeDtypeStruct(a.shape, a.dtype),
            jax.ShapeDtypeStruct((bp, GLA_HEADS, GLA_DK, GLA_DV), F32),
        ],
        scratch_shapes=[pltpu.VMEM((GLA_HEADS * GLA_DK, GLA_DV), F32)],
        input_output_aliases={6: 0},
        compiler_params=_cparams(("parallel", "arbitrary")),
        name=name,
    )(z, z, z, z, loga, gla_norm.reshape(1, GLA_DV), a)


def _gla_sample_kernel(q_ref, k_ref, v_ref, gate_ref, la_ref, gn_ref, s0_ref, a_hbm, o_ref, sn_ref, *, t, nseq):
    del a_hbm
    n = nseq * t
    npair = GLA_HEADS // 2
    gn = gn_ref[...]
    row = lax.broadcasted_iota(jnp.int32, (n, LANES), 0)
    lane = lax.broadcasted_iota(jnp.int32, (n, LANES), 1)
    head_lane = [lane < GLA_DK, lane >= GLA_DK]
    t_shift, lane_shift = t.bit_length() - 1, LANES.bit_length() - 1
    rg = row & (t - 1)
    t_i = lax.broadcasted_iota(jnp.int32, (n, n), 0)
    s_i = lax.broadcasted_iota(jnp.int32, (n, n), 1)
    m_seq = ((t_i >> t_shift) == (s_i >> t_shift)) & (s_i <= t_i)
    wide = nseq * LANES
    w_row = lax.broadcasted_iota(jnp.int32, (n, wide), 0)
    w_col = lax.broadcasted_iota(jnp.int32, (n, wide), 1)
    own_seq = (w_row >> t_shift) == (w_col >> lane_shift)
    own_last = own_seq & ((w_row & (t - 1)) == t - 1)
    w_head = [(w_col & GLA_DK) == 0, (w_col & GLA_DK) != 0]
    srow_head1 = (lax.broadcasted_iota(jnp.int32, (wide, GLA_DV), 0) & GLA_DK) != 0
    tile = lambda a: jnp.concatenate([a] * nseq, axis=1)

    for p in range(npair):
        qk = slice(p * LANES, (p + 1) * LANES)
        q = q_ref[:, qk] * (GLA_DK ** -0.5)
        k = k_ref[:, qk]
        b = _seg_cumsum(la_ref[:, qk], rg, t)
        r_first = _seg_first(b, rg, t)
        r_last = _seg_last(b, rg, t)
        q_sub = q * jnp.exp(b - r_first)
        k_sub = k * jnp.exp(r_first - b)
        q_state = q * jnp.exp(b)
        k_upd = k * jnp.exp(r_last - b)
        s_all = s0_ref[:, 2 * p:2 * p + 2].reshape(wide, GLA_DV)
        q_wide = jnp.where(own_seq, tile(q_state), 0.0)
        k_wide_t = jnp.where(own_seq, tile(k_upd), 0.0).T
        decay = jnp.exp(jnp.sum(jnp.where(own_last, tile(b), 0.0).T, axis=1, keepdims=True))
        upd = []
        for h in range(2):
            attn = jnp.where(m_seq, _dot_nt(jnp.where(head_lane[h], q_sub, 0.0), k_sub), 0.0)
            vcol = slice((2 * p + h) * GLA_DV, (2 * p + h + 1) * GLA_DV)
            vh = v_ref[:, vcol]
            o = _dot(jnp.where(w_head[h], q_wide, 0.0), s_all) + _dot(attn, vh)
            o_ref[:, vcol] = _gla_out(o, gate_ref[:, vcol], gn).astype(BF16)
            upd.append(_dot(k_wide_t, vh))
        s_new = decay * s_all + jnp.where(srow_head1, upd[1], upd[0])
        sn_ref[:, 2 * p:2 * p + 2] = s_new.reshape(nseq, 2, GLA_DK, GLA_DV)


def gla_sample(z, loga, gla_norm, cols, s0_all, layer, a, row0, bd, t, *, name):
    q0, k0, v0, g0 = cols
    nseq = 16 if bd % 16 == 0 else bd
    n = nseq * t
    assert n % 8 == 0 and row0 % n == 0 and t & (t - 1) == 0
    qk_w, v_w = GLA_HEADS * GLA_DK, GLA_HEADS * GLA_DV
    off = row0 // n
    return pl.pallas_call(
        functools.partial(_gla_sample_kernel, t=t, nseq=nseq),
        grid=(bd // nseq,),
        in_specs=[
            pl.BlockSpec((n, qk_w), lambda i: (i + off, q0 // qk_w)),
            pl.BlockSpec((n, qk_w), lambda i: (i + off, k0 // qk_w)),
            pl.BlockSpec((n, v_w), lambda i: (i + off, v0 // v_w)),
            pl.BlockSpec((n, v_w), lambda i: (i + off, g0 // v_w)),
            pl.BlockSpec((n, qk_w), lambda i: (i, 0)),
            pl.BlockSpec((1, GLA_DV), lambda i: (0, 0)),
            pl.BlockSpec((None, nseq, GLA_HEADS, GLA_DK, GLA_DV), lambda i: (layer, i, 0, 0, 0)),
            _ANY,
        ],
        out_specs=[
            pl.BlockSpec((n, v_w), lambda i: (i + off, 0)),
            pl.BlockSpec((nseq, GLA_HEADS, GLA_DK, GLA_DV), lambda i: (i, 0, 0, 0)),
        ],
        out_shape=[
            jax.ShapeDtypeStruct(a.shape, a.dtype),
            jax.ShapeDtypeStruct((bd, GLA_HEADS, GLA_DK, GLA_DV), F32),
        ],
        input_output_aliases={7: 0},
        compiler_params=_cparams(("parallel",)),
        name=name,
    )(z, z, z, z, loga, gla_norm.reshape(1, GLA_DV), s0_all, a)


def _mla_self_kernel(q_ref, c_ref, kpe_ref, w_ref, a_hbm, o_ref, k_scr, v_scr, *, tq, nq):
    del a_hbm
    i = pl.program_id(2)

    @pl.when(i == 0)
    def _():
        kv = _dot(c_ref[...].astype(BF16), w_ref[0])
        kpe = kpe_ref[...].astype(F32)
        for h in range(2):
            k_scr[h] = (kv[:, h * HEAD_PAD:(h + 1) * HEAD_PAD] + kpe).astype(BF16)
            v_scr[h] = kv[:, (2 + h) * HEAD_PAD:(3 + h) * HEAD_PAD].astype(BF16)

    t_i = lax.broadcasted_iota(jnp.int32, (tq, tq), 0)
    s_i = lax.broadcasted_iota(jnp.int32, (tq, tq), 1)
    causal = s_i <= t_i
    qs = [q_ref[:, h * HEAD_PAD:(h + 1) * HEAD_PAD] for h in range(2)]

    def step(kb, carry, masked):
        r0 = kb * tq
        new = []
        for h in range(2):
            m, l, acc = carry[h]
            s = _dot_nt(qs[h], k_scr[h, r0:r0 + tq, :])
            if masked:
                s = jnp.where(causal, s, -jnp.inf)
            m_new = jnp.maximum(m, jnp.max(s, axis=1, keepdims=True))
            alpha = jnp.exp2(m - m_new)
            p = jnp.exp2(s - m_new)
            l = alpha * l + jnp.sum(p, axis=1, keepdims=True)
            acc = alpha * acc + _dot(p.astype(BF16), v_scr[h, r0:r0 + tq, :])
            new.append((m_new, l, acc))
        return tuple(new)

    init1 = (jnp.full((tq, 1), -jnp.inf, F32), jnp.zeros((tq, 1), F32), jnp.zeros((tq, HEAD_PAD), F32))

    def run(nfull):
        carry = (init1, init1)
        for kb in range(nfull):
            carry = step(kb, carry, False)
        (_, l0, acc0), (_, l1, acc1) = step(nfull, carry, True)
        o_ref[...] = (acc0 / l0 + acc1 / l1).astype(BF16)

    for iq in range(nq):
        pl.when(i == iq)(functools.partial(run, iq))


def mla_self(qp, c_new, kpe, wkv_pair, a, col0, bp, s, *, name):
    tq = _row_tile(s, 512)
    nq = s // tq
    npair = MLA_HEADS // 2
    kvl = c_new.shape[1]
    ow = 2 * MLA_V
    assert col0 % ow == 0
    return pl.pallas_call(
        functools.partial(_mla_self_kernel, tq=tq, nq=nq),
        grid=(bp, npair, nq),
        in_specs=[
            pl.BlockSpec((tq, 2 * HEAD_PAD), lambda b, p, i: (b * nq + i, p)),
            pl.BlockSpec((s, kvl), lambda b, p, i: (b, 0)),
            pl.BlockSpec((s, LANES), lambda b, p, i: (b, 0)),
            pl.BlockSpec((1, kvl, 4 * HEAD_PAD), lambda b, p, i: (p, 0, 0)),
            _ANY,
        ],
        out_specs=pl.BlockSpec((tq, ow), lambda b, p, i: (b * nq + i, col0 // ow + p)),
        out_shape=jax.ShapeDtypeStruct(a.shape, a.dtype),
        scratch_shapes=[pltpu.VMEM((2, s, HEAD_PAD), BF16), pltpu.VMEM((2, s, HEAD_PAD), BF16)],
        input_output_aliases={4: 0},
        compiler_params=_cparams(("parallel", "parallel", "arbitrary")),
        name=name,
    )(qp, c_new, kpe, wkv_pair, a)


def _q_absorb_kernel(q_ref, w_ref, o_ref):
    o_ref[0] = _dot(q_ref[...], w_ref[0])


def q_absorb(qp, wq_abs, *, name):
    rows = qp.shape[0]
    wout = wq_abs.shape[2]
    return pl.pallas_call(
        _q_absorb_kernel,
        grid=(MLA_HEADS,),
        in_specs=[
            pl.BlockSpec((rows, HEAD_PAD), lambda h: (0, h)),
            pl.BlockSpec((1, HEAD_PAD, wout), lambda h: (h, 0, 0)),
        ],
        out_specs=pl.BlockSpec((1, rows, wout), lambda h: (h, 0, 0)),
        out_shape=jax.ShapeDtypeStruct((MLA_HEADS, rows, wout), F32),
        compiler_params=_cparams(("parallel",)),
        name=name,
    )(qp, wq_abs)


def _mla_cached_kernel(pt_ref, q_ref, cn_ref, krn_ref, ckv_hbm, ckr_hbm, o_ref, kv_buf, kr_buf, sem,
                       *, layer, npages, page, t, kvl, nchunk, group):
    b = pl.program_id(0)
    nb = pl.num_programs(0)
    slot = b % 2
    cw = npages // nchunk * page

    def copies(step, sl, u, p):
        pg = pt_ref[step * group + u, p]
        rows = pl.ds(p * page, page)
        return (pltpu.make_async_copy(ckv_hbm.at[layer, pg], kv_buf.at[sl, u, rows], sem.at[0, sl]),
                pltpu.make_async_copy(ckr_hbm.at[layer, pg], kr_buf.at[sl, u, :, rows], sem.at[1, sl]))

    def start_all(step, sl):
        for u in range(group):
            for p in range(npages):
                for cp in copies(step, sl, u, p):
                    cp.start()

    @pl.when(b == 0)
    def _():
        start_all(0, 0)

    @pl.when(b + 1 < nb)
    def _():
        start_all(b + 1, 1 - slot)

    n = q_ref.shape[1]
    tok = lax.broadcasted_iota(jnp.int32, (n, 1), 0) & (t - 1)
    state = []
    for u in range(group):
        q = q_ref[u]
        q_lat = q[:, :kvl]
        q_pe = q[:, kvl:kvl + MLA_ROPE]
        c_new = cn_ref[u]
        kr_new = krn_ref[u]
        s_new = []
        for jn in range(t):
            sj = (jnp.sum(q_lat * c_new[jn:jn + 1, :], axis=1, keepdims=True)
                  + jnp.sum(q_pe * kr_new[jn:jn + 1, :], axis=1, keepdims=True))
            s_new.append(jnp.where(tok >= jn, sj, -jnp.inf))
        m = s_new[0]
        for sj in s_new[1:]:
            m = jnp.maximum(m, sj)
        l = jnp.zeros((n, 1), F32)
        o = jnp.zeros((n, kvl), F32)
        for sj, jn in zip(s_new, range(t)):
            pj = jnp.exp2(sj - m)
            l = l + pj
            o = o + pj * c_new[jn:jn + 1, :]
        q_rows = jnp.concatenate([q_lat.astype(BF16), jnp.zeros((LANES - n, kvl), BF16)], axis=0)
        state.append((m, l, o, q_rows, q_pe))
    for u in range(group):
        for p in range(npages):
            for cp in copies(b, slot, u, p):
                cp.wait()
    for ch in range(nchunk):
        for u in range(group):
            m, l, o, q_rows, q_pe = state[u]
            past_c = kv_buf[slot, u, ch * cw:(ch + 1) * cw, :].astype(BF16)
            past_rt = kr_buf[slot, u, :, ch * cw:(ch + 1) * cw]
            s = _dot_nt(past_c, q_rows).T[:n] + _dot(q_pe, past_rt)
            m_new = jnp.maximum(m, jnp.max(s, axis=1, keepdims=True))
            alpha = jnp.exp2(m - m_new)
            p_c = jnp.exp2(s - m_new)
            l = alpha * l + jnp.sum(p_c, axis=1, keepdims=True)
            o = alpha * o + _dot(p_c.astype(BF16), past_c)
            state[u] = (m_new, l, o, q_rows, q_pe)
    for u in range(group):
        _, l, o, _, _ = state[u]
        o_ref[u] = o / l


def mla_cached(page_table, q_abs, c_new_s, kr_new_s, cache_kv, cache_kr_t, layer, *, name):
    bd, npages = page_table.shape
    page, kvl = cache_kv.shape[2], cache_kv.shape[3]
    rope = cache_kr_t.shape[2]
    t = c_new_s.shape[1]
    n, qw = q_abs.shape[1], q_abs.shape[2]
    past = npages * page
    nchunk = 4 if npages % 4 == 0 else 1
    group = 1
    assert t & (t - 1) == 0 and n <= LANES
    grid_spec = pltpu.PrefetchScalarGridSpec(
        num_scalar_prefetch=1,
        grid=(bd // group,),
        in_specs=[
            pl.BlockSpec((group, n, qw), lambda b, pt: (b, 0, 0)),
            pl.BlockSpec((group, t, kvl), lambda b, pt: (b, 0, 0)),
            pl.BlockSpec((group, t, rope), lambda b, pt: (b, 0, 0)),
            _ANY,
            _ANY,
        ],
        out_specs=pl.BlockSpec((group, n, kvl), lambda b, pt: (b, 0, 0)),
        scratch_shapes=[
            pltpu.VMEM((2, group, past, kvl), F32),
            pltpu.VMEM((2, group, rope, past), F32),
            pltpu.SemaphoreType.DMA((2, 2)),
        ],
    )
    return pl.pallas_call(
        functools.partial(_mla_cached_kernel, layer=layer, npages=npages, page=page, t=t, kvl=kvl,
                          nchunk=nchunk, group=group),
        grid_spec=grid_spec,
        out_shape=jax.ShapeDtypeStruct((bd, n, kvl), F32),
        compiler_params=_cparams(("arbitrary",)),
        name=name,
    )(page_table, q_abs, c_new_s, kr_new_s, cache_kv, cache_kr_t)


def _v_up_kernel(o_ref, w_ref, a_hbm, out_ref):
    del a_hbm
    out_ref[...] = (_dot(o_ref[0].astype(BF16), w_ref[0, 0]) + _dot(o_ref[1].astype(BF16), w_ref[0, 1])).astype(BF16)


def v_up(o_lat, wv_pair, a, row0, col0, *, name):
    h, rows, kvl = o_lat.shape
    ow = 2 * MLA_V
    assert row0 % rows == 0 and col0 % ow == 0
    return pl.pallas_call(
        _v_up_kernel,
        grid=(h // 2,),
        in_specs=[
            pl.BlockSpec((2, rows, kvl), lambda p: (p, 0, 0)),
            pl.BlockSpec((1, 2, kvl, ow), lambda p: (p, 0, 0, 0)),
            _ANY,
        ],
        out_specs=pl.BlockSpec((rows, ow), lambda p: (row0 // rows, col0 // ow + p)),
        out_shape=jax.ShapeDtypeStruct(a.shape, a.dtype),
        input_output_aliases={2: 0},
        compiler_params=_cparams(("parallel",)),
        name=name,
    )(o_lat, wv_pair, a)


CONV_PAD = 32
CONV_SUB = 64
CONV_LANES = 256


def _ln_swish(y, lg, lb):
    mu = jnp.mean(y, axis=-1, keepdims=True)
    yc = y - mu
    var = jnp.mean(yc * yc, axis=-1, keepdims=True)
    yn = yc * lax.rsqrt(var + EPS) * lg + lb
    return yn * _sigmoid(yn)


def _conv_prompt_kernel(u_ref, halo_ref, x_ref, wdw_ref, bdw_ref, lg_ref, lb_ref, w2_ref, b2_ref,
                        o_ref, st_ref, ext_ref, sh_ref, h_ref, *, tt, cw):
    hist = cw - 1
    lo = CONV_PAD - hist
    d = u_ref.shape[1]
    j = pl.program_id(1)

    @pl.when(j == 0)
    def _():
        ext_ref[0:CONV_PAD, :] = jnp.zeros((CONV_PAD, d), F32)

    @pl.when(j > 0)
    def _():
        ext_ref[0:CONV_PAD, :] = halo_ref[...]

    ext_ref[CONV_PAD:CONV_PAD + tt, :] = u_ref[...]
    st_ref[0] = ext_ref[CONV_PAD + tt - hist:CONV_PAD + tt, :]
    span = tt + CONV_PAD - SUBLANES
    for r in range(1, SUBLANES):
        for c0 in range(0, d, CONV_LANES):
            sh_ref[r - 1, 0:span, c0:c0 + CONV_LANES] = ext_ref[r:r + span, c0:c0 + CONV_LANES]

    grp = (CONV_SUB // SUBLANES, SUBLANES, LANES)
    for c0 in range(0, d, LANES):
        cs = slice(c0, c0 + LANES)
        wts = [wdw_ref[w, :, cs] for w in range(cw)]
        bias = jnp.broadcast_to(bdw_ref[:, cs].reshape(1, 1, LANES), grp)

        for r0 in range(0, tt, CONV_SUB):
            acc = bias
            for w in range(cw):
                r, a = (w + lo) % SUBLANES, (w + lo) // SUBLANES
                rows = slice(r0 + a * SUBLANES, r0 + a * SUBLANES + CONV_SUB)
                src = ext_ref[rows, cs] if r == 0 else sh_ref[r - 1, rows, cs]
                acc = acc + src.reshape(grp) * wts[w][None]
            h_ref[r0:r0 + CONV_SUB, cs] = acc.reshape(CONV_SUB, LANES)

    for r0 in range(0, tt, CONV_SUB):
        h_ref[r0:r0 + CONV_SUB, :] = _ln_swish(h_ref[r0:r0 + CONV_SUB, :], lg_ref[...], lb_ref[...])
    o_ref[...] = x_ref[...] + b2_ref[...] + _dot(h_ref[...].astype(BF16), w2_ref[...])


def conv_prompt(u, x, w_dw, b_dw, ln_g, ln_b, w2_bf16, layer, b2, bp, s, *, name):
    r, d = x.shape
    cw = w_dw.shape[0]
    hist = cw - 1
    tt = _row_tile(s, 512)
    assert hist <= CONV_PAD and tt % CONV_SUB == 0 and tt % CONV_PAD == 0 and d % CONV_LANES == 0
    nj = s // tt
    hb = tt // CONV_PAD
    blk = pl.BlockSpec((tt, d), lambda b, j: (b * nj + j, 0))
    return pl.pallas_call(
        functools.partial(_conv_prompt_kernel, tt=tt, cw=cw),
        grid=(bp, nj),
        in_specs=[
            blk,
            pl.BlockSpec((CONV_PAD, d), lambda b, j: (jnp.maximum((b * nj + j) * hb - 1, 0), 0)),
            blk,
            _const_spec((cw, SUBLANES, d)), _const_spec((1, d)), _const_spec((1, d)), _const_spec((1, d)),
            _layer_spec((d, d), layer), _const_spec((1, d)),
        ],
        out_specs=[blk, pl.BlockSpec((1, hist, d), lambda b, j: (b, 0, 0))],
        out_shape=[jax.ShapeDtypeStruct((r, d), F32), jax.ShapeDtypeStruct((bp, hist, d), F32)],
        scratch_shapes=[
            pltpu.VMEM((tt + CONV_PAD, d), F32),
            pltpu.VMEM((SUBLANES - 1, tt + CONV_PAD, d), F32),
            pltpu.VMEM((tt, d), F32),
        ],
        input_output_aliases={2: 0},
        compiler_params=_cparams(("parallel", "arbitrary")),
        name=name,
    )(u, u, x, jnp.broadcast_to(w_dw[:, None, :], (cw, SUBLANES, d)), b_dw.reshape(1, d), ln_g.reshape(1, d),
      ln_b.reshape(1, d), w2_bf16, b2.reshape(1, d))


def _conv_sample_kernel(u_ref, buf_ref, x_ref, wdw_ref, bdw_ref, lg_ref, lb_ref, w2_ref, b2_ref,
                        o_ref, st_ref, ext_ref, h_ref, *, t, bb, cw):
    hist = cw - 1
    lo = CONV_PAD - hist
    for i in range(bb):
        ext_ref[i, lo:CONV_PAD, :] = buf_ref[i]
        ext_ref[i, CONV_PAD:CONV_PAD + t, :] = u_ref[i * t:(i + 1) * t, :]
        st_ref[i] = ext_ref[i, CONV_PAD + t - hist:CONV_PAD + t, :]
        acc = jnp.broadcast_to(bdw_ref[...], (t, bdw_ref.shape[1]))
        for w in range(cw):
            acc = acc + ext_ref[i, lo + w:lo + w + t, :] * wdw_ref[w:w + 1, :]
        h_ref[i * t:(i + 1) * t, :] = _ln_swish(acc, lg_ref[...], lb_ref[...])
    o_ref[...] = x_ref[...] + b2_ref[...] + _dot(h_ref[...].astype(BF16), w2_ref[...])


def conv_sample(u, buf, x, w_dw, b_dw, ln_g, ln_b, w2_bf16, layer, b2, row0, bd, t, *, name):
    r, d = x.shape
    cw = w_dw.shape[0]
    hist = cw - 1
    bb = 32 if bd % 32 == 0 else bd
    rows = bb * t
    assert hist <= CONV_PAD and rows % 8 == 0 and row0 % rows == 0
    off = row0 // rows
    blk = pl.BlockSpec((rows, d), lambda i: (i + off, 0))
    ext_rows = -(-(CONV_PAD + t) // SUBLANES) * SUBLANES
    return pl.pallas_call(
        functools.partial(_conv_sample_kernel, t=t, bb=bb, cw=cw),
        grid=(bd // bb,),
        in_specs=[
            blk,
            pl.BlockSpec((bb, hist, d), lambda i: (i, 0, 0)),
            blk,
            _const_spec((cw, d)), _const_spec((1, d)), _const_spec((1, d)), _const_spec((1, d)),
            _layer_spec((d, d), layer), _const_spec((1, d)),
        ],
        out_specs=[blk, pl.BlockSpec((bb, hist, d), lambda i: (i, 0, 0))],
        out_shape=[jax.ShapeDtypeStruct((r, d), F32), jax.ShapeDtypeStruct((bd, hist, d), F32)],
        scratch_shapes=[pltpu.VMEM((bb, ext_rows, d), F32), pltpu.VMEM((rows, d), F32)],
        input_output_aliases={2: 0},
        compiler_params=_cparams(("parallel",)),
        name=name,
    )(u, buf, x, w_dw, b_dw.reshape(1, d), ln_g.reshape(1, d), ln_b.reshape(1, d), w2_bf16, b2.reshape(1, d))


def _rope_tables(pos):
    inv = jnp.power(ROPE_THETA, -jnp.arange(0, MLA_ROPE, 2, dtype=F32) / MLA_ROPE)
    ang = pos.astype(F32)[:, None] * inv[None, :]
    cos, sin = jnp.cos(ang), jnp.sin(ang)
    n = pos.shape[0]
    lo = ROPE_LO
    cos_t = jnp.ones((n, LANES), F32).at[:, lo:lo + MLA_ROPE].set(jnp.concatenate([cos, cos], axis=1))
    sin_a = jnp.zeros((n, LANES), F32).at[:, lo:lo + ROPE_HALF].set(-sin)
    sin_b = jnp.zeros((n, LANES), F32).at[:, lo + ROPE_HALF:lo + MLA_ROPE].set(sin)
    return cos_t, sin_a, sin_b


def _layout_w_in(w_in, gla_qk, gla_v, q_lora, kv_lora):
    nl, d = w_in.shape[0], w_in.shape[1]
    sp = [0, gla_qk, 2 * gla_qk, 2 * gla_qk + gla_v, 2 * gla_qk + gla_v + GLA_RANK]
    sp += [sp[-1] + gla_v, sp[-1] + gla_v + q_lora, sp[-1] + gla_v + q_lora + kv_lora]
    q, k, v, a, g, dq, dkv, kr = [w_in[..., lo:hi] for lo, hi in zip(sp, sp[1:] + [w_in.shape[2]])]
    last = jnp.zeros((nl, d, LANES), w_in.dtype).at[..., :GLA_RANK].set(a)
    last = last.at[..., ROPE_LO:ROPE_LO + MLA_ROPE].set(kr)
    w = jnp.concatenate([q, k, v, g, dq, dkv, last], axis=2)
    cols = dict(q=0, k=gla_qk, v=2 * gla_qk, g=2 * gla_qk + gla_v, dq=2 * gla_qk + 2 * gla_v)
    cols["dkv"] = cols["dq"] + q_lora
    cols["akr"] = cols["dkv"] + kv_lora
    return w.astype(BF16), cols


def _layout_w_uq(w_uq):
    ql = w_uq.shape[0]
    w = w_uq.reshape(ql, MLA_HEADS, MLA_NOPE + MLA_ROPE)
    w = jnp.pad(w, ((0, 0), (0, 0), (0, HEAD_PAD - MLA_NOPE - MLA_ROPE)))
    return w.reshape(ql, MLA_HEADS * HEAD_PAD).astype(BF16)


def _layout_w_ukv(w_ukv):
    kvl = w_ukv.shape[0]
    w = w_ukv.reshape(kvl, MLA_HEADS, MLA_NOPE + MLA_V)
    wk = jnp.pad(w[..., :MLA_NOPE], ((0, 0), (0, 0), (0, HEAD_PAD - MLA_NOPE)))
    wv = w[..., MLA_NOPE:]
    wv_e = jnp.pad(wv[:, 0::2], ((0, 0), (0, 0), (0, MLA_V)))
    wv_o = jnp.pad(wv[:, 1::2], ((0, 0), (0, 0), (MLA_V, 0)))
    pair = jnp.stack([wk[:, 0::2], wk[:, 1::2], wv_e, wv_o], axis=2)
    pair = pair.transpose(1, 0, 2, 3).reshape(MLA_HEADS // 2, kvl, 4 * HEAD_PAD)
    w_uk_t = w[..., :MLA_NOPE].transpose(1, 2, 0)
    wq_abs = jnp.zeros((MLA_HEADS, HEAD_PAD, kvl + LANES), F32)
    wq_abs = wq_abs.at[:, :MLA_NOPE, :kvl].set(w_uk_t)
    wq_abs = wq_abs.at[:, ROPE_LO:ROPE_LO + MLA_ROPE, kvl:kvl + MLA_ROPE].set(jnp.eye(MLA_ROPE, dtype=F32))
    wv_h = wv.transpose(1, 0, 2)
    wv_pair = jnp.stack([jnp.pad(wv_h[0::2], ((0, 0), (0, 0), (0, MLA_V))),
                         jnp.pad(wv_h[1::2], ((0, 0), (0, 0), (MLA_V, 0)))], axis=1)
    return pair.astype(BF16), wq_abs.astype(BF16), wv_pair.astype(BF16)


def kernel(x_prompt, x_sample, cache_kv, cache_kr, state_gla, state_conv, page_table, norm_mix, norm_mlp,
           norm_final, w_in, w_gate_a2, b_gate_a, gla_norm, mla_q_norm, mla_kv_norm, w_uq, w_ukv, w_out_ab,
           w_pw1, b_pw1, w_dw, b_dw, conv_ln_g, conv_ln_b, w_pw2, b_pw2, w_up, w_down):
    bp, s, d = x_prompt.shape
    bd, t, _ = x_sample.shape
    rp, rs = bp * s, bd * t
    r = rp + rs
    depth = norm_mix.shape[0]
    page = cache_kv.shape[2]
    past = page_table.shape[1] * page
    gla_qk, gla_v = GLA_HEADS * GLA_DK, GLA_HEADS * GLA_DV
    mla_w = MLA_HEADS * MLA_V
    q_lora, kv_lora = mla_q_norm.shape[1], mla_kv_norm.shape[1]
    d_ff = w_up.shape[2]
    tf = 512 if d_ff % 512 == 0 else d_ff

    tabs_p = _rope_tables(jnp.arange(s))
    tabs_s = _rope_tables(jnp.tile(past + jnp.arange(t), bd))
    cache_kr_t = jnp.swapaxes(cache_kr, 2, 3)

    w_in_all, cols = _layout_w_in(w_in, gla_qk, gla_v, q_lora, kv_lora)
    zero_bias = jnp.zeros((w_in_all.shape[2],), F32)
    w_pw1_all, w_pw2_all, w_out_all = w_pw1.astype(BF16), w_pw2.astype(BF16), w_out_ab.astype(BF16)
    w_up_all = w_up.astype(BF16)
    w_down_all = w_down.astype(BF16).reshape(depth, d_ff // tf, tf, d)

    x = (x_prompt.reshape(rp, d), x_sample.reshape(rs, d))
    kv_p, kr_p, gla_p, conv_p, kv_s, kr_s, gla_s, conv_s = ([] for _ in range(8))
    for l in range(depth):
        i = l // 2
        if l % 2 == 0:
            z = norm_matmul(x, norm_mix[l], w_in_all, i, zero_bias, name=f"in_proj{l}")
            w2_pad = jnp.zeros((LANES, gla_qk), F32).at[:GLA_RANK].set(w_gate_a2[i]).astype(BF16)
            pcols = (cols["dq"], cols["dkv"], cols["akr"])
            pargs = (mla_q_norm[i], mla_kv_norm[i], _layout_w_uq(w_uq[i]), w2_pad, b_gate_a[i])
            loga_p, qp_p, c_p, krn_p, kpe_p = ab_proj(z, pcols, tabs_p, 0, rp, *pargs, name=f"ab_proj_prompt{l}")
            loga_s, qp_s, c_s, krn_s, _ = ab_proj(z, pcols, tabs_s, rp, rs, *pargs, name=f"ab_proj_sample{l}")
            gcols = (cols["q"], cols["k"], cols["v"], cols["g"])
            a = jnp.zeros((r, gla_v + mla_w), BF16)
            a, sg_p = gla_prompt(z, loga_p, gla_norm[i], gcols, a, bp, s, name=f"gla_prompt{l}")
            a, sg_s = gla_sample(z, loga_s, gla_norm[i], gcols, state_gla, i, a, rp, bd, t, name=f"gla_sample{l}")
            wkv_pair, wq_abs, wv_pair = _layout_w_ukv(w_ukv[i])
            a = mla_self(qp_p, c_p, kpe_p, wkv_pair, a, gla_v, bp, s, name=f"mla_self{l}")
            q_abs = q_absorb(qp_s, wq_abs, name=f"q_absorb{l}")
            q_abs = q_abs.reshape(MLA_HEADS, bd, t, -1).transpose(1, 0, 2, 3).reshape(bd, MLA_HEADS * t, -1)
            c_s3 = c_s.reshape(bd, t, kv_lora)
            krn_s3 = krn_s.reshape(bd, t, MLA_ROPE)
            o_lat = mla_cached(page_table, q_abs, c_s3, krn_s3, cache_kv, cache_kr_t, i, name=f"mla_cached{l}")
            o_lat = o_lat.reshape(bd, MLA_HEADS, t, kv_lora).transpose(1, 0, 2, 3).reshape(MLA_HEADS, rs, kv_lora)
            a = v_up(o_lat, wv_pair, a, rp, gla_v, name=f"v_up{l}")
            x = out_proj_residual(a, w_out_all, i, x, name=f"out_proj{l}")
            kv_p.append(c_p.reshape(bp, s, kv_lora)); kr_p.append(krn_p.reshape(bp, s, MLA_ROPE))
            kv_s.append(c_s3); kr_s.append(krn_s3)
            gla_p.append(sg_p); gla_s.append(sg_s)
        else:
            u = norm_matmul(x, norm_mix[l], w_pw1_all, i, b_pw1[i], glu=True, name=f"pw1_glu{l}")
            cargs = (w_dw[i], b_dw[i], conv_ln_g[i], conv_ln_b[i], w_pw2_all, i, b_pw2[i])
            x, st_p = conv_prompt(u, x, *cargs, bp, s, name=f"conv_prompt{l}")
            x, st_s = conv_sample(u, state_conv[i], x, *cargs, rp, bd, t, name=f"conv_sample{l}")
            conv_p.append(st_p); conv_s.append(st_s)
        if l + 1 < depth:
            x = mlp_residual(x, norm_mlp[l], w_up_all, w_down_all, l, name=f"mlp{l}")
    margs = (x, norm_mlp[depth - 1], w_up_all, w_down_all, depth - 1)
    y_p = mlp_residual(*margs, 0, rp, norm_final, name="mlp_final_prompt").reshape(bp, s, d)
    y_s = mlp_residual(*margs, rp, rs, norm_final, name="mlp_final_sample").reshape(bd, t, d)
    return (y_p, y_s, jnp.stack(kv_p), jnp.stack(kr_p), jnp.stack(gla_p), jnp.stack(conv_p),
            jnp.stack(kv_s), jnp.stack(kr_s), jnp.stack(gla_s), jnp.stack(conv_s))
```

```python
import functools
import math

import jax
import jax.numpy as jnp
from jax import lax
from jax.experimental import pallas as pl
from jax.experimental.pallas import tpu as pltpu

F32 = jnp.float32
BF16 = jnp.bfloat16

GLA_HEADS = 4
GLA_DK = 64
GLA_DV = 128
GLA_RANK = 16
GLA_TAU = 16.0
GLA_CHUNK = 64
GLA_SUB = 16
MLA_HEADS = 8
MLA_NOPE = 64
MLA_ROPE = 32
MLA_V = 64
MLA_SCALE = (MLA_NOPE + MLA_ROPE) ** -0.5
Q_PRESCALE = MLA_SCALE * math.log2(math.e)
ROPE_THETA = 10000.0
EPS = 1e-6

LANES = 128
SUBLANES = 8
HEAD_PAD = 128
ROPE_LO = MLA_NOPE
ROPE_HALF = MLA_ROPE // 2
VMEM_LIMIT = 56 * 1024 * 1024


def _cparams(sem):
    return pltpu.CompilerParams(dimension_semantics=sem, vmem_limit_bytes=VMEM_LIMIT)


def _row_tile(n, target):
    best = None
    for t in range(8, min(n, target) + 1, 8):
        if n % t == 0:
            best = t
    assert best is not None, n
    return best


def _const_spec(shape):
    nd = len(shape)
    return pl.BlockSpec(shape, lambda *_: (0,) * nd, pipeline_mode=pl.Buffered(1))


def _layer_spec(shape, layer):
    nd = len(shape)
    return pl.BlockSpec((None,) + tuple(shape), lambda *_: (layer,) + (0,) * nd, pipeline_mode=pl.Buffered(1))


_ANY = pl.BlockSpec(memory_space=pl.ANY)


def _rms(x, g):
    ms = jnp.mean(x * x, axis=-1, keepdims=True)
    return x * lax.rsqrt(ms + EPS) * g


def _sigmoid(x):
    return 1.0 / (1.0 + jnp.exp(-x))


def _dot(a, b):
    return jnp.dot(a, b, preferred_element_type=F32)


def _dot_nt(a, b):
    return lax.dot_general(a, b, (((1,), (1,)), ((), ())), preferred_element_type=F32)


def _split_row_specs(tm, d, n_first):
    return [pl.BlockSpec((tm, d), lambda i: (jnp.minimum(i, n_first - 1), 0)),
            pl.BlockSpec((tm, d), lambda i: (jnp.maximum(i - n_first, 0), 0))]


def _split_row_tile(first_ref, second_ref, n_first):
    return jnp.where(pl.program_id(0) < n_first, first_ref[...], second_ref[...])


def _row_sources(x, target):
    if isinstance(x, tuple):
        (r0, d), r1 = x[0].shape, x[1].shape[0]
        tm = _row_tile(math.gcd(r0, r1), target)
        return x, r0 + r1, d, tm, r0 // tm
    return (x,), x.shape[0], x.shape[1], _row_tile(x.shape[0], target), None


def _norm_matmul_kernel(*refs, glu, n_first):
    if n_first is None:
        x_ref, g_ref, w_ref, b_ref, o_ref = refs
        x = x_ref[...]
    else:
        xa_ref, xb_ref, g_ref, w_ref, b_ref, o_ref = refs
        x = _split_row_tile(xa_ref, xb_ref, n_first)
    h = _rms(x, g_ref[...]).astype(BF16)
    u = _dot(h, w_ref[...]) + b_ref[...]
    if glu:
        half = u.shape[1] // 2
        u = u[:, :half] * _sigmoid(u[:, half:])
    o_ref[...] = u


def norm_matmul(x, g, w_bf16, layer, bias, *, glu=False, name):
    xs, r, d, tm, n_first = _row_sources(x, 1024)
    n = w_bf16.shape[2]
    n_out = n // 2 if glu else n
    x_specs = [pl.BlockSpec((tm, d), lambda i: (i, 0))] if n_first is None else _split_row_specs(tm, d, n_first)
    return pl.pallas_call(
        functools.partial(_norm_matmul_kernel, glu=glu, n_first=n_first),
        grid=(r // tm,),
        in_specs=x_specs + [
            _const_spec((1, d)),
            _layer_spec((d, n), layer),
            _const_spec((1, n)),
        ],
        out_specs=pl.BlockSpec((tm, n_out), lambda i: (i, 0)),
        out_shape=jax.ShapeDtypeStruct((r, n_out), F32),
        compiler_params=_cparams(("parallel",)),
        name=name,
    )(*xs, g.reshape(1, d), w_bf16, bias.reshape(1, n))


def _mlp_kernel(x_ref, g_ref, wup_ref, wdn_ref, *rest, final):
    if final:
        fg_ref, o_ref, h_ref = rest
    else:
        o_ref, h_ref = rest
    x = x_ref[...]
    h_ref[...] = _rms(x, g_ref[...]).astype(BF16)
    o_ref[...] = x

    nf, tf = wdn_ref.shape[0], wdn_ref.shape[1]

    def body(f, carry):
        c0 = pl.multiple_of(f * tf, tf)
        u = jnp.maximum(_dot(h_ref[...], wup_ref[:, pl.ds(c0, tf)]), 0.0)
        o_ref[...] += _dot((u * u).astype(BF16), wdn_ref[f])
        return carry

    lax.fori_loop(0, nf, body, 0, unroll=2 if nf % 2 == 0 else 1)
    if final:
        o_ref[...] = _rms(o_ref[...], fg_ref[...])


def mlp_residual(x, g, wup, wdn4, layer, row0=0, rows=None, final_g=None, *, name):
    d = x.shape[1]
    rows = x.shape[0] if rows is None else rows
    _, nf, tf, _ = wdn4.shape
    tm = _row_tile(math.gcd(row0, rows) if row0 else rows, 1024)
    off = row0 // tm
    final = final_g is not None
    extra_specs, extra = ([_const_spec((1, d))], [final_g.reshape(1, d)]) if final else ([], [])
    return pl.pallas_call(
        functools.partial(_mlp_kernel, final=final),
        grid=(rows // tm,),
        in_specs=[
            pl.BlockSpec((tm, d), lambda i: (i + off, 0)),
            _const_spec((1, d)),
            _layer_spec((d, nf * tf), layer),
            _layer_spec((nf, tf, d), layer),
        ] + extra_specs,
        out_specs=pl.BlockSpec((tm, d), lambda i: (i, 0)),
        out_shape=jax.ShapeDtypeStruct((rows, d), F32),
        scratch_shapes=[pltpu.VMEM((tm, d), BF16)],
        compiler_params=_cparams(("parallel",)),
        name=name,
    )(x, g.reshape(1, d), wup, wdn4, *extra)


def _out_proj_kernel(a_ref, w_ref, *rest, n_first):
    if n_first is None:
        r_ref, o_ref = rest
        res = r_ref[...]
    else:
        ra_ref, rb_ref, o_ref = rest
        res = _split_row_tile(ra_ref, rb_ref, n_first)
    o_ref[...] = res + _dot(a_ref[...], w_ref[...])


def out_proj_residual(a_bf16, w_bf16, layer, res, *, name):
    k = a_bf16.shape[1]
    rs, r, d, tm, n_first = _row_sources(res, 1024)
    r_specs = [pl.BlockSpec((tm, d), lambda i: (i, 0))] if n_first is None else _split_row_specs(tm, d, n_first)
    return pl.pallas_call(
        functools.partial(_out_proj_kernel, n_first=n_first),
        grid=(r // tm,),
        in_specs=[
            pl.BlockSpec((tm, k), lambda i: (i, 0)),
            _layer_spec((k, d), layer),
        ] + r_specs,
        out_specs=pl.BlockSpec((tm, d), lambda i: (i, 0)),
        out_shape=jax.ShapeDtypeStruct((r, d), F32),
        compiler_params=_cparams(("parallel",)),
        name=name,
    )(a_bf16, w_bf16, *rs)


def _rope_block(x, cos_t, sin_a, sin_b):
    up = pltpu.roll(x, LANES - ROPE_HALF, 1)
    dn = pltpu.roll(x, ROPE_HALF, 1)
    return x * cos_t + up * sin_a + dn * sin_b


def _ab_proj_kernel(dq_ref, dkv_ref, akr_ref, cos_ref, sa_ref, sb_ref, qn_ref, kvn_ref,
                    wuq_ref, w2_ref, b2_ref, loga_ref, q_ref, c_ref, kr_ref, kpe_ref):
    cos_t, sin_a, sin_b = cos_ref[...], sa_ref[...], sb_ref[...]
    cq = _rms(dq_ref[...], qn_ref[...]).astype(BF16)
    qm = _dot(cq, wuq_ref[...])
    for h in range(MLA_HEADS):
        blk = qm[:, h * HEAD_PAD:(h + 1) * HEAD_PAD]
        roped = _rope_block(blk, cos_t, sin_a, sin_b)
        q_ref[:, h * HEAD_PAD:(h + 1) * HEAD_PAD] = (roped * Q_PRESCALE).astype(BF16)
    c_ref[...] = _rms(dkv_ref[...], kvn_ref[...])
    akr = akr_ref[...]
    y = _rope_block(akr, cos_t, sin_a, sin_b)
    lane = lax.broadcasted_iota(jnp.int32, y.shape, 1)
    rope_lane = (lane >= ROPE_LO) & (lane < ROPE_LO + MLA_ROPE)
    kpe_ref[...] = jnp.where(rope_lane, y, 0.0).astype(BF16)
    kr_ref[...] = y[:, ROPE_LO:ROPE_LO + MLA_ROPE]
    xa = _dot(akr.astype(BF16), w2_ref[...]) + b2_ref[...]
    log_sig = jnp.minimum(xa, 0.0) - jnp.log(1.0 + jnp.exp(-jnp.abs(xa)))
    loga_ref[...] = log_sig * (1.0 / GLA_TAU)


def ab_proj(z, cols, tabs, row0, rows, q_norm, kv_norm, wuq_pad, w2_pad, layer, b2, *, name):
    tab_rows = tabs[0].shape[0]
    tm = _row_tile(math.gcd(math.gcd(row0, rows) if row0 else rows, tab_rows), 512)
    off = row0 // tm
    ntab = tab_rows // tm
    dq0, dkv0, akr0 = cols
    ql, kvl = q_norm.shape[0], kv_norm.shape[0]
    nq = wuq_pad.shape[2]
    gw = w2_pad.shape[2]
    out_row = lambda w: pl.BlockSpec((tm, w), lambda i: (i, 0))
    tab = pl.BlockSpec((tm, LANES), lambda i: (i % ntab, 0))
    return pl.pallas_call(
        _ab_proj_kernel,
        grid=(rows // tm,),
        in_specs=[
            pl.BlockSpec((tm, ql), lambda i: (i + off, dq0 // ql)),
            pl.BlockSpec((tm, kvl), lambda i: (i + off, dkv0 // kvl)),
            pl.BlockSpec((tm, LANES), lambda i: (i + off, akr0 // LANES)),
            tab, tab, tab,
            _const_spec((1, ql)), _const_spec((1, kvl)),
            _layer_spec((ql, nq), layer), _layer_spec((LANES, gw), layer), _const_spec((1, gw)),
        ],
        out_specs=[out_row(gw), out_row(nq), out_row(kvl), out_row(MLA_ROPE), out_row(LANES)],
        out_shape=[
            jax.ShapeDtypeStruct((rows, gw), F32),
            jax.ShapeDtypeStruct((rows, nq), BF16),
            jax.ShapeDtypeStruct((rows, kvl), F32),
            jax.ShapeDtypeStruct((rows, MLA_ROPE), F32),
            jax.ShapeDtypeStruct((rows, LANES), BF16),
        ],
        compiler_params=_cparams(("parallel",)),
        name=name,
    )(z, z, z, *tabs, q_norm.reshape(1, ql), kv_norm.reshape(1, kvl), wuq_pad, w2_pad, b2.reshape(1, gw))


def _seg_cumsum(x, rg, group):
    k = 1
    while k < group:
        x = x + jnp.where(rg >= k, pltpu.roll(x, k, 0), 0.0)
        k *= 2
    return x


def _seg_first(x, rg, group):
    y = jnp.where(rg == 0, x, 0.0)
    k = 1
    while k < group:
        y = y + pltpu.roll(y, k, 0)
        k *= 2
    return y


def _seg_last(x, rg, group):
    n = x.shape[0]
    y = jnp.where(rg == group - 1, x, 0.0)
    k = 1
    while k < group:
        y = y + pltpu.roll(y, n - k, 0)
        k *= 2
    return y


def _gla_out(o, gate, gn):
    return _rms(o, gn) * (gate * _sigmoid(gate))


def _gla_prompt_kernel(q_ref, k_ref, v_ref, gate_ref, la_ref, gn_ref, a_hbm, o_ref, sf_ref, s_ref, *, nchunk):
    del a_hbm
    c = GLA_CHUNK
    npair = GLA_HEADS // 2
    j = pl.program_id(1)

    @pl.when(j == 0)
    def _():
        s_ref[...] = jnp.zeros_like(s_ref)

    gn = gn_ref[...]
    row = lax.broadcasted_iota(jnp.int32, (c, LANES), 0)
    lane = lax.broadcasted_iota(jnp.int32, (c, LANES), 1)
    head_lane = [lane < GLA_DK, lane >= GLA_DK]
    rg = row & (GLA_SUB - 1)
    t_i = lax.broadcasted_iota(jnp.int32, (c, c), 0)
    s_i = lax.broadcasted_iota(jnp.int32, (c, c), 1)
    sub_shift = GLA_SUB.bit_length() - 1
    tb, sb = t_i >> sub_shift, s_i >> sub_shift
    m_diag = (tb == sb) & (s_i <= t_i)
    m_next = (tb == sb + 1) & ((sb & 1) == 0)
    m_half = (t_i >= c // 2) & (s_i < c // 2)

    def chunk(ci, carry):
        r0 = pl.multiple_of(ci * c, c)
        for p in range(npair):
            qk = slice(p * LANES, (p + 1) * LANES)
            q = q_ref[pl.ds(r0, c), qk] * (GLA_DK ** -0.5)
            k = k_ref[pl.ds(r0, c), qk]
            b = _seg_cumsum(la_ref[pl.ds(r0, c), qk], row, c)
            r_sub = _seg_first(b, rg, GLA_SUB)
            r_next = pltpu.roll(r_sub, c - GLA_SUB, 0)
            b_half = b[c // 2:c // 2 + 1, :]
            q_sub = q * jnp.exp(b - r_sub)
            k_sub = k * jnp.exp(r_sub - b)
            k_next = k * jnp.exp(jnp.minimum(r_next - b, 0.0))
            q_half = q * jnp.exp(jnp.minimum(b - b_half, 0.0))
            k_half = k * jnp.exp(jnp.minimum(b_half - b, 0.0))
            q_state = q * jnp.exp(b)
            k_t = k.T
            b_t = b.T
            b_last = b_t[:, c - 1:c]
            k_upd = k_t * jnp.exp(b_last - b_t)
            s_decay = jnp.exp(b_last)
            s_pair = s_ref[p * LANES:(p + 1) * LANES, :]
            for h in range(2):
                hm = head_lane[h]
                z = lambda a: jnp.where(hm, a, 0.0)
                attn = (jnp.where(m_diag, _dot_nt(z(q_sub), k_sub), 0.0)
                        + jnp.where(m_next, _dot_nt(z(q_sub), k_next), 0.0)
                        + jnp.where(m_half, _dot_nt(z(q_half), k_half), 0.0))
                vcol = slice((2 * p + h) * GLA_DV, (2 * p + h + 1) * GLA_DV)
                vh = v_ref[pl.ds(r0, c), vcol]
                o = _dot(z(q_state), s_pair) + _dot(attn, vh)
                o_ref[pl.ds(r0, c), vcol] = _gla_out(o, gate_ref[pl.ds(r0, c), vcol], gn).astype(BF16)
                lo, hi = h * GLA_DK, (h + 1) * GLA_DK
                srow = slice(p * LANES + lo, p * LANES + hi)
                s_ref[srow, :] = s_decay[lo:hi] * s_pair[lo:hi] + _dot(k_upd[lo:hi], vh)
        return carry

    lax.fori_loop(0, nchunk, chunk, 0, unroll=8 if nchunk % 8 == 0 else 1)

    @pl.when(j == pl.num_programs(1) - 1)
    def _():
        for h in range(GLA_HEADS):
            sf_ref[0, h] = s_ref[h * GLA_DK:(h + 1) * GLA_DK, :]


def gla_prompt(z, loga, gla_norm, cols, a, bp, s, *, name):
    q0, k0, v0, g0 = cols
    tb = _row_tile(s, 512)
    assert tb % GLA_CHUNK == 0
    nj = s // tb
    qk_w, v_w = GLA_HEADS * GLA_DK, GLA_HEADS * GLA_DV
    rowblk = lambda b, j: b * nj + j
    return pl.pallas_call(
        functools.partial(_gla_prompt_kernel, nchunk=tb // GLA_CHUNK),
        grid=(bp, nj),
        in_specs=[
            pl.BlockSpec((tb, qk_w), lambda b, j: (rowblk(b, j), q0 // qk_w)),
            pl.BlockSpec((tb, qk_w), lambda b, j: (rowblk(b, j), k0 // qk_w)),
            pl.BlockSpec((tb, v_w), lambda b, j: (rowblk(b, j), v0 // v_w)),
            pl.BlockSpec((tb, v_w), lambda b, j: (rowblk(b, j), g0 // v_w)),
            pl.BlockSpec((tb, qk_w), lambda b, j: (rowblk(b, j), 0)),
            pl.BlockSpec((1, GLA_DV), lambda b, j: (0, 0)),
            _ANY,
        ],
        out_specs=[
            pl.BlockSpec((tb, v_w), lambda b, j: (rowblk(b, j), 0)),
            pl.BlockSpec((1, GLA_HEADS, GLA_DK, GLA_DV), lambda b, j: (b, 0, 0, 0)),
        ],
        out_shape=[
            jax.ShapeDtypeStruct(a.shape, a.dtype),
            jax.ShapeDtypeStruct((bp, GLA_HEADS, GLA_DK, GLA_DV), F32),
        ],
        scratch_shapes=[pltpu.VMEM((GLA_HEADS * GLA_DK, GLA_DV), F32)],
        input_output_aliases={6: 0},
        compiler_params=_cparams(("parallel", "arbitrary")),
        name=name,
    )(z, z, z, z, loga, gla_norm.reshape(1, GLA_DV), a)


def _gla_sample_kernel(q_ref, k_ref, v_ref, gate_ref, la_ref, gn_ref, s0_ref, a_hbm, o_ref, sn_ref, *, t, nseq):
    del a_hbm
    n = nseq * t
    npair = GLA_HEADS // 2
    gn = gn_ref[...]
    row = lax.broadcasted_iota(jnp.int32, (n, LANES), 0)
    lane = lax.broadcasted_iota(jnp.int32, (n, LANES), 1)
    head_lane = [lane < GLA_DK, lane >= GLA_DK]
    t_shift, lane_shift = t.bit_length() - 1, LANES.bit_length() - 1
    rg = row & (t - 1)
    t_i = lax.broadcasted_iota(jnp.int32, (n, n), 0)
    s_i = lax.broadcasted_iota(jnp.int32, (n, n), 1)
    m_seq = ((t_i >> t_shift) == (s_i >> t_shift)) & (s_i <= t_i)
    wide = nseq * LANES
    w_row = lax.broadcasted_iota(jnp.int32, (n, wide), 0)
    w_col = lax.broadcasted_iota(jnp.int32, (n, wide), 1)
    own_seq = (w_row >> t_shift) == (w_col >> lane_shift)
    own_last = own_seq & ((w_row & (t - 1)) == t - 1)
    w_head = [(w_col & GLA_DK) == 0, (w_col & GLA_DK) != 0]
    srow_head1 = (lax.broadcasted_iota(jnp.int32, (wide, GLA_DV), 0) & GLA_DK) != 0
    tile = lambda a: jnp.concatenate([a] * nseq, axis=1)

    for p in range(npair):
        qk = slice(p * LANES, (p + 1) * LANES)
        q = q_ref[:, qk] * (GLA_DK ** -0.5)
        k = k_ref[:, qk]
        b = _seg_cumsum(la_ref[:, qk], rg, t)
        r_first = _seg_first(b, rg, t)
        r_last = _seg_last(b, rg, t)
        q_sub = q * jnp.exp(b - r_first)
        k_sub = k * jnp.exp(r_first - b)
        q_state = q * jnp.exp(b)
        k_upd = k * jnp.exp(r_last - b)
        s_all = s0_ref[:, 2 * p:2 * p + 2].reshape(wide, GLA_DV)
        q_wide = jnp.where(own_seq, tile(q_state), 0.0)
        k_wide_t = jnp.where(own_seq, tile(k_upd), 0.0).T
        decay = jnp.exp(jnp.sum(jnp.where(own_last, tile(b), 0.0).T, axis=1, keepdims=True))
        upd = []
        for h in range(2):
            attn = jnp.where(m_seq, _dot_nt(jnp.where(head_lane[h], q_sub, 0.0), k_sub), 0.0)
            vcol = slice((2 * p + h) * GLA_DV, (2 * p + h + 1) * GLA_DV)
            vh = v_ref[:, vcol]
            o = _dot(jnp.where(w_head[h], q_wide, 0.0), s_all) + _dot(attn, vh)
            o_ref[:, vcol] = _gla_out(o, gate_ref[:, vcol], gn).astype(BF16)
            upd.append(_dot(k_wide_t, vh))
        s_new = decay * s_all + jnp.where(srow_head1, upd[1], upd[0])
        sn_ref[:, 2 * p:2 * p + 2] = s_new.reshape(nseq, 2, GLA_DK, GLA_DV)


def gla_sample(z, loga, gla_norm, cols, s0_all, layer, a, row0, bd, t, *, name):
    q0, k0, v0, g0 = cols
    nseq = 16 if bd % 16 == 0 else bd
    n = nseq * t
    assert n % 8 == 0 and row0 % n == 0 and t & (t - 1) == 0
    qk_w, v_w = GLA_HEADS * GLA_DK, GLA_HEADS * GLA_DV
    off = row0 // n
    return pl.pallas_call(
        functools.partial(_gla_sample_kernel, t=t, nseq=nseq),
        grid=(bd // nseq,),
        in_specs=[
            pl.BlockSpec((n, qk_w), lambda i: (i + off, q0 // qk_w)),
            pl.BlockSpec((n, qk_w), lambda i: (i + off, k0 // qk_w)),
            pl.BlockSpec((n, v_w), lambda i: (i + off, v0 // v_w)),
            pl.BlockSpec((n, v_w), lambda i: (i + off, g0 // v_w)),
            pl.BlockSpec((n, qk_w), lambda i: (i, 0)),
            pl.BlockSpec((1, GLA_DV), lambda i: (0, 0)),
            pl.BlockSpec((None, nseq, GLA_HEADS, GLA_DK, GLA_DV), lambda i: (layer, i, 0, 0, 0)),
            _ANY,
        ],
        out_specs=[
            pl.BlockSpec((n, v_w), lambda i: (i + off, 0)),
            pl.BlockSpec((nseq, GLA_HEADS, GLA_DK, GLA_DV), lambda i: (i, 0, 0, 0)),
        ],
        out_shape=[
            jax.ShapeDtypeStruct(a.shape, a.dtype),
            jax.ShapeDtypeStruct((bd, GLA_HEADS, GLA_DK, GLA_DV), F32),
        ],
        input_output_aliases={7: 0},
        compiler_params=_cparams(("parallel",)),
        name=name,
    )(z, z, z, z, loga, gla_norm.reshape(1, GLA_DV), s0_all, a)


def _mla_self_kernel(q_ref, c_ref, kpe_ref, w_ref, a_hbm, o_ref, k_scr, v_scr, *, tq, nq):
    del a_hbm
    i = pl.program_id(2)

    @pl.when(i == 0)
    def _():
        kv = _dot(c_ref[...].astype(BF16), w_ref[0])
        kpe = kpe_ref[...].astype(F32)
        for h in range(2):
            k_scr[h] = (kv[:, h * HEAD_PAD:(h + 1) * HEAD_PAD] + kpe).astype(BF16)
            v_scr[h] = kv[:, (2 + h) * HEAD_PAD:(3 + h) * HEAD_PAD].astype(BF16)

    t_i = lax.broadcasted_iota(jnp.int32, (tq, tq), 0)
    s_i = lax.broadcasted_iota(jnp.int32, (tq, tq), 1)
    causal = s_i <= t_i
    qs = [q_ref[:, h * HEAD_PAD:(h + 1) * HEAD_PAD] for h in range(2)]

    def step(kb, carry, masked):
        r0 = kb * tq
        new = []
        for h in range(2):
            m, l, acc = carry[h]
            s = _dot_nt(qs[h], k_scr[h, r0:r0 + tq, :])
            if masked:
                s = jnp.where(causal, s, -jnp.inf)
            m_new = jnp.maximum(m, jnp.max(s, axis=1, keepdims=True))
            alpha = jnp.exp2(m - m_new)
            p = jnp.exp2(s - m_new)
            l = alpha * l + jnp.sum(p, axis=1, keepdims=True)
            acc = alpha * acc + _dot(p.astype(BF16), v_scr[h, r0:r0 + tq, :])
            new.append((m_new, l, acc))
        return tuple(new)

    init1 = (jnp.full((tq, 1), -jnp.inf, F32), jnp.zeros((tq, 1), F32), jnp.zeros((tq, HEAD_PAD), F32))

    def run(nfull):
        carry = (init1, init1)
        for kb in range(nfull):
            carry = step(kb, carry, False)
        (_, l0, acc0), (_, l1, acc1) = step(nfull, carry, True)
        o_ref[...] = (acc0 / l0 + acc1 / l1).astype(BF16)

    for iq in range(nq):
        pl.when(i == iq)(functools.partial(run, iq))


def mla_self(qp, c_new, kpe, wkv_pair, layer, a, col0, bp, s, *, name):
    tq = _row_tile(s, 512)
    nq = s // tq
    npair = MLA_HEADS // 2
    kvl = c_new.shape[1]
    ow = 2 * MLA_V
    assert col0 % ow == 0
    return pl.pallas_call(
        functools.partial(_mla_self_kernel, tq=tq, nq=nq),
        grid=(bp, npair, nq),
        in_specs=[
            pl.BlockSpec((tq, 2 * HEAD_PAD), lambda b, p, i: (b * nq + i, p)),
            pl.BlockSpec((s, kvl), lambda b, p, i: (b, 0)),
            pl.BlockSpec((s, LANES), lambda b, p, i: (b, 0)),
            pl.BlockSpec((None, 1, kvl, 4 * HEAD_PAD), lambda b, p, i: (layer, p, 0, 0)),
            _ANY,
        ],
        out_specs=pl.BlockSpec((tq, ow), lambda b, p, i: (b * nq + i, col0 // ow + p)),
        out_shape=jax.ShapeDtypeStruct(a.shape, a.dtype),
        scratch_shapes=[pltpu.VMEM((2, s, HEAD_PAD), BF16), pltpu.VMEM((2, s, HEAD_PAD), BF16)],
        input_output_aliases={4: 0},
        compiler_params=_cparams(("parallel", "parallel", "arbitrary")),
        name=name,
    )(qp, c_new, kpe, wkv_pair, a)


def _q_absorb_kernel(q_ref, w_ref, o_ref):
    o_ref[0] = _dot(q_ref[...], w_ref[0])


def q_absorb(qp, wq_abs, layer, *, name):
    rows = qp.shape[0]
    wout = wq_abs.shape[3]
    return pl.pallas_call(
        _q_absorb_kernel,
        grid=(MLA_HEADS,),
        in_specs=[
            pl.BlockSpec((rows, HEAD_PAD), lambda h: (0, h)),
            pl.BlockSpec((None, 1, HEAD_PAD, wout), lambda h: (layer, h, 0, 0)),
        ],
        out_specs=pl.BlockSpec((1, rows, wout), lambda h: (h, 0, 0)),
        out_shape=jax.ShapeDtypeStruct((MLA_HEADS, rows, wout), F32),
        compiler_params=_cparams(("parallel",)),
        name=name,
    )(qp, wq_abs)


def _mla_cached_kernel(pt_ref, q_ref, cn_ref, krn_ref, ckv_hbm, ckr_hbm, o_ref, kv_buf, kr_buf, sem,
                       *, layer, npages, page, t, kvl, nchunk, group):
    b = pl.program_id(0)
    nb = pl.num_programs(0)
    slot = b % 2
    cw = npages // nchunk * page

    def copies(step, sl, u, p):
        pg = pt_ref[step * group + u, p]
        rows = pl.ds(p * page, page)
        return (pltpu.make_async_copy(ckv_hbm.at[layer, pg], kv_buf.at[sl, u, rows], sem.at[0, sl]),
                pltpu.make_async_copy(ckr_hbm.at[layer, pg], kr_buf.at[sl, u, :, rows], sem.at[1, sl]))

    def start_all(step, sl):
        for u in range(group):
            for p in range(npages):
                for cp in copies(step, sl, u, p):
                    cp.start()

    @pl.when(b == 0)
    def _():
        start_all(0, 0)

    @pl.when(b + 1 < nb)
    def _():
        start_all(b + 1, 1 - slot)

    n = q_ref.shape[1]
    tok = lax.broadcasted_iota(jnp.int32, (n, 1), 0) & (t - 1)
    state = []
    for u in range(group):
        q = q_ref[u]
        q_lat = q[:, :kvl]
        q_pe = q[:, kvl:kvl + MLA_ROPE]
        c_new = cn_ref[u]
        kr_new = krn_ref[u]
        s_new = []
        for jn in range(t):
            sj = (jnp.sum(q_lat * c_new[jn:jn + 1, :], axis=1, keepdims=True)
                  + jnp.sum(q_pe * kr_new[jn:jn + 1, :], axis=1, keepdims=True))
            s_new.append(jnp.where(tok >= jn, sj, -jnp.inf))
        m = s_new[0]
        for sj in s_new[1:]:
            m = jnp.maximum(m, sj)
        l = jnp.zeros((n, 1), F32)
        o = jnp.zeros((n, kvl), F32)
        for sj, jn in zip(s_new, range(t)):
            pj = jnp.exp2(sj - m)
            l = l + pj
            o = o + pj * c_new[jn:jn + 1, :]
        q_rows = jnp.concatenate([q_lat.astype(BF16), jnp.zeros((LANES - n, kvl), BF16)], axis=0)
        state.append((m, l, o, q_rows, q_pe))
    for u in range(group):
        for p in range(npages):
            for cp in copies(b, slot, u, p):
                cp.wait()
    for ch in range(nchunk):
        for u in range(group):
            m, l, o, q_rows, q_pe = state[u]
            past_c = kv_buf[slot, u, ch * cw:(ch + 1) * cw, :].astype(BF16)
            past_rt = kr_buf[slot, u, :, ch * cw:(ch + 1) * cw]
            s = _dot_nt(past_c, q_rows).T[:n] + _dot(q_pe, past_rt)
            m_new = jnp.maximum(m, jnp.max(s, axis=1, keepdims=True))
            alpha = jnp.exp2(m - m_new)
            p_c = jnp.exp2(s - m_new)
            l = alpha * l + jnp.sum(p_c, axis=1, keepdims=True)
            o = alpha * o + _dot(p_c.astype(BF16), past_c)
            state[u] = (m_new, l, o, q_rows, q_pe)
    for u in range(group):
        _, l, o, _, _ = state[u]
        o_ref[u] = o / l


def mla_cached(page_table, q_abs, c_new_s, kr_new_s, cache_kv, cache_kr_t, layer, *, name):
    bd, npages = page_table.shape
    page, kvl = cache_kv.shape[2], cache_kv.shape[3]
    rope = cache_kr_t.shape[2]
    t = c_new_s.shape[1]
    n, qw = q_abs.shape[1], q_abs.shape[2]
    past = npages * page
    nchunk = 4 if npages % 4 == 0 else 1
    group = 1
    assert t & (t - 1) == 0 and n <= LANES
    grid_spec = pltpu.PrefetchScalarGridSpec(
        num_scalar_prefetch=1,
        grid=(bd // group,),
        in_specs=[
            pl.BlockSpec((group, n, qw), lambda b, pt: (b, 0, 0)),
            pl.BlockSpec((group, t, kvl), lambda b, pt: (b, 0, 0)),
            pl.BlockSpec((group, t, rope), lambda b, pt: (b, 0, 0)),
            _ANY,
            _ANY,
        ],
        out_specs=pl.BlockSpec((group, n, kvl), lambda b, pt: (b, 0, 0)),
        scratch_shapes=[
            pltpu.VMEM((2, group, past, kvl), F32),
            pltpu.VMEM((2, group, rope, past), F32),
            pltpu.SemaphoreType.DMA((2, 2)),
        ],
    )
    return pl.pallas_call(
        functools.partial(_mla_cached_kernel, layer=layer, npages=npages, page=page, t=t, kvl=kvl,
                          nchunk=nchunk, group=group),
        grid_spec=grid_spec,
        out_shape=jax.ShapeDtypeStruct((bd, n, kvl), F32),
        compiler_params=_cparams(("arbitrary",)),
        name=name,
    )(page_table, q_abs, c_new_s, kr_new_s, cache_kv, cache_kr_t)


def _v_up_kernel(o_ref, w_ref, a_hbm, out_ref):
    del a_hbm
    out_ref[...] = (_dot(o_ref[0].astype(BF16), w_ref[0, 0]) + _dot(o_ref[1].astype(BF16), w_ref[0, 1])).astype(BF16)


def v_up(o_lat, wv_pair, layer, a, row0, col0, *, name):
    h, rows, kvl = o_lat.shape
    ow = 2 * MLA_V
    assert row0 % rows == 0 and col0 % ow == 0
    return pl.pallas_call(
        _v_up_kernel,
        grid=(h // 2,),
        in_specs=[
            pl.BlockSpec((2, rows, kvl), lambda p: (p, 0, 0)),
            pl.BlockSpec((None, 1, 2, kvl, ow), lambda p: (layer, p, 0, 0, 0)),
            _ANY,
        ],
        out_specs=pl.BlockSpec((rows, ow), lambda p: (row0 // rows, col0 // ow + p)),
        out_shape=jax.ShapeDtypeStruct(a.shape, a.dtype),
        input_output_aliases={2: 0},
        compiler_params=_cparams(("parallel",)),
        name=name,
    )(o_lat, wv_pair, a)


CONV_PAD = 32
CONV_SUB = 64
CONV_LANES = 256


def _ln_swish(y, lg, lb):
    mu = jnp.mean(y, axis=-1, keepdims=True)
    yc = y - mu
    var = jnp.mean(yc * yc, axis=-1, keepdims=True)
    yn = yc * lax.rsqrt(var + EPS) * lg + lb
    return yn * _sigmoid(yn)


def _conv_prompt_kernel(u_ref, halo_ref, x_ref, wdw_ref, bdw_ref, lg_ref, lb_ref, w2_ref, b2_ref,
                        o_ref, st_ref, ext_ref, sh_ref, h_ref, *, tt, cw):
    hist = cw - 1
    lo = CONV_PAD - hist
    d = u_ref.shape[1]
    j = pl.program_id(1)

    @pl.when(j == 0)
    def _():
        ext_ref[0:CONV_PAD, :] = jnp.zeros((CONV_PAD, d), F32)

    @pl.when(j > 0)
    def _():
        ext_ref[0:CONV_PAD, :] = halo_ref[...]

    ext_ref[CONV_PAD:CONV_PAD + tt, :] = u_ref[...]
    st_ref[0] = ext_ref[CONV_PAD + tt - hist:CONV_PAD + tt, :]
    span = tt + CONV_PAD - SUBLANES
    for r in range(1, SUBLANES):
        for c0 in range(0, d, CONV_LANES):
            sh_ref[r - 1, 0:span, c0:c0 + CONV_LANES] = ext_ref[r:r + span, c0:c0 + CONV_LANES]

    grp = (CONV_SUB // SUBLANES, SUBLANES, LANES)
    for c0 in range(0, d, LANES):
        cs = slice(c0, c0 + LANES)
        wts = [wdw_ref[w, :, cs] for w in range(cw)]
        bias = jnp.broadcast_to(bdw_ref[:, cs].reshape(1, 1, LANES), grp)

        for r0 in range(0, tt, CONV_SUB):
            acc = bias
            for w in range(cw):
                r, a = (w + lo) % SUBLANES, (w + lo) // SUBLANES
                rows = slice(r0 + a * SUBLANES, r0 + a * SUBLANES + CONV_SUB)
                src = ext_ref[rows, cs] if r == 0 else sh_ref[r - 1, rows, cs]
                acc = acc + src.reshape(grp) * wts[w][None]
            h_ref[r0:r0 + CONV_SUB, cs] = acc.reshape(CONV_SUB, LANES)

    for r0 in range(0, tt, CONV_SUB):
        h_ref[r0:r0 + CONV_SUB, :] = _ln_swish(h_ref[r0:r0 + CONV_SUB, :], lg_ref[...], lb_ref[...])
    o_ref[...] = x_ref[...] + b2_ref[...] + _dot(h_ref[...].astype(BF16), w2_ref[...])


def conv_prompt(u, x, w_dw, b_dw, ln_g, ln_b, w2_bf16, layer, b2, bp, s, *, name):
    r, d = x.shape
    cw = w_dw.shape[0]
    hist = cw - 1
    tt = _row_tile(s, 512)
    assert hist <= CONV_PAD and tt % CONV_SUB == 0 and tt % CONV_PAD == 0 and d % CONV_LANES == 0
    nj = s // tt
    hb = tt // CONV_PAD
    blk = pl.BlockSpec((tt, d), lambda b, j: (b * nj + j, 0))
    return pl.pallas_call(
        functools.partial(_conv_prompt_kernel, tt=tt, cw=cw),
        grid=(bp, nj),
        in_specs=[
            blk,
            pl.BlockSpec((CONV_PAD, d), lambda b, j: (jnp.maximum((b * nj + j) * hb - 1, 0), 0)),
            blk,
            _const_spec((cw, SUBLANES, d)), _const_spec((1, d)), _const_spec((1, d)), _const_spec((1, d)),
            _layer_spec((d, d), layer), _const_spec((1, d)),
        ],
        out_specs=[blk, pl.BlockSpec((1, hist, d), lambda b, j: (b, 0, 0))],
        out_shape=[jax.ShapeDtypeStruct((r, d), F32), jax.ShapeDtypeStruct((bp, hist, d), F32)],
        scratch_shapes=[
            pltpu.VMEM((tt + CONV_PAD, d), F32),
            pltpu.VMEM((SUBLANES - 1, tt + CONV_PAD, d), F32),
            pltpu.VMEM((tt, d), F32),
        ],
        input_output_aliases={2: 0},
        compiler_params=_cparams(("parallel", "arbitrary")),
        name=name,
    )(u, u, x, jnp.broadcast_to(w_dw[:, None, :], (cw, SUBLANES, d)), b_dw.reshape(1, d), ln_g.reshape(1, d),
      ln_b.reshape(1, d), w2_bf16, b2.reshape(1, d))


def _conv_sample_kernel(u_ref, buf_ref, x_ref, wdw_ref, bdw_ref, lg_ref, lb_ref, w2_ref, b2_ref,
                        o_ref, st_ref, ext_ref, h_ref, *, t, bb, cw):
    hist = cw - 1
    lo = CONV_PAD - hist
    for i in range(bb):
        ext_ref[i, lo:CONV_PAD, :] = buf_ref[i]
        ext_ref[i, CONV_PAD:CONV_PAD + t, :] = u_ref[i * t:(i + 1) * t, :]
        st_ref[i] = ext_ref[i, CONV_PAD + t - hist:CONV_PAD + t, :]
        acc = jnp.broadcast_to(bdw_ref[...], (t, bdw_ref.shape[1]))
        for w in range(cw):
            acc = acc + ext_ref[i, lo + w:lo + w + t, :] * wdw_ref[w:w + 1, :]
        h_ref[i * t:(i + 1) * t, :] = _ln_swish(acc, lg_ref[...], lb_ref[...])
    o_ref[...] = x_ref[...] + b2_ref[...] + _dot(h_ref[...].astype(BF16), w2_ref[...])


def conv_sample(u, buf, x, w_dw, b_dw, ln_g, ln_b, w2_bf16, layer, b2, row0, bd, t, *, name):
    r, d = x.shape
    cw = w_dw.shape[0]
    hist = cw - 1
    bb = 32 if bd % 32 == 0 else bd
    rows = bb * t
    assert hist <= CONV_PAD and rows % 8 == 0 and row0 % rows == 0
    off = row0 // rows
    blk = pl.BlockSpec((rows, d), lambda i: (i + off, 0))
    ext_rows = -(-(CONV_PAD + t) // SUBLANES) * SUBLANES
    return pl.pallas_call(
        functools.partial(_conv_sample_kernel, t=t, bb=bb, cw=cw),
        grid=(bd // bb,),
        in_specs=[
            blk,
            pl.BlockSpec((bb, hist, d), lambda i: (i, 0, 0)),
            blk,
            _const_spec((cw, d)), _const_spec((1, d)), _const_spec((1, d)), _const_spec((1, d)),
            _layer_spec((d, d), layer), _const_spec((1, d)),
        ],
        out_specs=[blk, pl.BlockSpec((bb, hist, d), lambda i: (i, 0, 0))],
        out_shape=[jax.ShapeDtypeStruct((r, d), F32), jax.ShapeDtypeStruct((bd, hist, d), F32)],
        scratch_shapes=[pltpu.VMEM((bb, ext_rows, d), F32), pltpu.VMEM((rows, d), F32)],
        input_output_aliases={2: 0},
        compiler_params=_cparams(("parallel",)),
        name=name,
    )(u, buf, x, w_dw, b_dw.reshape(1, d), ln_g.reshape(1, d), ln_b.reshape(1, d), w2_bf16, b2.reshape(1, d))


def _rope_tables(pos):
    inv = jnp.power(ROPE_THETA, -jnp.arange(0, MLA_ROPE, 2, dtype=F32) / MLA_ROPE)
    ang = pos.astype(F32)[:, None] * inv[None, :]
    cos, sin = jnp.cos(ang), jnp.sin(ang)
    n = pos.shape[0]
    lo = ROPE_LO
    cos_t = jnp.ones((n, LANES), F32).at[:, lo:lo + MLA_ROPE].set(jnp.concatenate([cos, cos], axis=1))
    sin_a = jnp.zeros((n, LANES), F32).at[:, lo:lo + ROPE_HALF].set(-sin)
    sin_b = jnp.zeros((n, LANES), F32).at[:, lo + ROPE_HALF:lo + MLA_ROPE].set(sin)
    return cos_t, sin_a, sin_b


def _layout_w_in(w_in, gla_qk, gla_v, q_lora, kv_lora):
    nl, d = w_in.shape[0], w_in.shape[1]
    sp = [0, gla_qk, 2 * gla_qk, 2 * gla_qk + gla_v, 2 * gla_qk + gla_v + GLA_RANK]
    sp += [sp[-1] + gla_v, sp[-1] + gla_v + q_lora, sp[-1] + gla_v + q_lora + kv_lora]
    q, k, v, a, g, dq, dkv, kr = [w_in[..., lo:hi] for lo, hi in zip(sp, sp[1:] + [w_in.shape[2]])]
    last = jnp.zeros((nl, d, LANES), w_in.dtype).at[..., :GLA_RANK].set(a)
    last = last.at[..., ROPE_LO:ROPE_LO + MLA_ROPE].set(kr)
    w = jnp.concatenate([q, k, v, g, dq, dkv, last], axis=2)
    cols = dict(q=0, k=gla_qk, v=2 * gla_qk, g=2 * gla_qk + gla_v, dq=2 * gla_qk + 2 * gla_v)
    cols["dkv"] = cols["dq"] + q_lora
    cols["akr"] = cols["dkv"] + kv_lora
    return w.astype(BF16), cols


def _layout_w_uq(w_uq):
    ql = w_uq.shape[0]
    w = w_uq.reshape(ql, MLA_HEADS, MLA_NOPE + MLA_ROPE)
    w = jnp.pad(w, ((0, 0), (0, 0), (0, HEAD_PAD - MLA_NOPE - MLA_ROPE)))
    return w.reshape(ql, MLA_HEADS * HEAD_PAD).astype(BF16)


def _layout_w_ukv(w_ukv):
    kvl = w_ukv.shape[0]
    w = w_ukv.reshape(kvl, MLA_HEADS, MLA_NOPE + MLA_V)
    wk = jnp.pad(w[..., :MLA_NOPE], ((0, 0), (0, 0), (0, HEAD_PAD - MLA_NOPE)))
    wv = w[..., MLA_NOPE:]
    wv_e = jnp.pad(wv[:, 0::2], ((0, 0), (0, 0), (0, MLA_V)))
    wv_o = jnp.pad(wv[:, 1::2], ((0, 0), (0, 0), (MLA_V, 0)))
    pair = jnp.stack([wk[:, 0::2], wk[:, 1::2], wv_e, wv_o], axis=2)
    pair = pair.transpose(1, 0, 2, 3).reshape(MLA_HEADS // 2, kvl, 4 * HEAD_PAD)
    w_uk_t = w[..., :MLA_NOPE].transpose(1, 2, 0)
    wq_abs = jnp.zeros((MLA_HEADS, HEAD_PAD, kvl + LANES), F32)
    wq_abs = wq_abs.at[:, :MLA_NOPE, :kvl].set(w_uk_t)
    wq_abs = wq_abs.at[:, ROPE_LO:ROPE_LO + MLA_ROPE, kvl:kvl + MLA_ROPE].set(jnp.eye(MLA_ROPE, dtype=F32))
    wv_h = wv.transpose(1, 0, 2)
    wv_pair = jnp.stack([jnp.pad(wv_h[0::2], ((0, 0), (0, 0), (0, MLA_V))),
                         jnp.pad(wv_h[1::2], ((0, 0), (0, 0), (MLA_V, 0)))], axis=1)
    return pair.astype(BF16), wq_abs.astype(BF16), wv_pair.astype(BF16)


def kernel(x_prompt, x_sample, cache_kv, cache_kr, state_gla, state_conv, page_table, norm_mix, norm_mlp,
           norm_final, w_in, w_gate_a2, b_gate_a, gla_norm, mla_q_norm, mla_kv_norm, w_uq, w_ukv, w_out_ab,
           w_pw1, b_pw1, w_dw, b_dw, conv_ln_g, conv_ln_b, w_pw2, b_pw2, w_up, w_down):
    bp, s, d = x_prompt.shape
    bd, t, _ = x_sample.shape
    rp, rs = bp * s, bd * t
    r = rp + rs
    depth = norm_mix.shape[0]
    page = cache_kv.shape[2]
    past = page_table.shape[1] * page
    gla_qk, gla_v = GLA_HEADS * GLA_DK, GLA_HEADS * GLA_DV
    mla_w = MLA_HEADS * MLA_V
    q_lora, kv_lora = mla_q_norm.shape[1], mla_kv_norm.shape[1]
    d_ff = w_up.shape[2]
    tf = 512 if d_ff % 512 == 0 else d_ff

    tabs_p = _rope_tables(jnp.arange(s))
    tabs_s = _rope_tables(jnp.tile(past + jnp.arange(t), bd))
    cache_kr_t = jnp.swapaxes(cache_kr, 2, 3)

    w_in_all, cols = _layout_w_in(w_in, gla_qk, gla_v, q_lora, kv_lora)
    zero_bias = jnp.zeros((w_in_all.shape[2],), F32)
    w_pw1_all, w_pw2_all, w_out_all = w_pw1.astype(BF16), w_pw2.astype(BF16), w_out_ab.astype(BF16)
    w_up_all = w_up.astype(BF16)
    wuq_all = jax.vmap(_layout_w_uq)(w_uq)
    wkv_all, wq_abs_all, wv_pair_all = jax.vmap(_layout_w_ukv)(w_ukv)
    w2_all = jnp.zeros((w_gate_a2.shape[0], LANES, gla_qk), F32).at[:, :GLA_RANK].set(w_gate_a2).astype(BF16)
    w_down_all = w_down.astype(BF16).reshape(depth, d_ff // tf, tf, d)

    x = (x_prompt.reshape(rp, d), x_sample.reshape(rs, d))
    kv_p, kr_p, gla_p, conv_p, kv_s, kr_s, gla_s, conv_s = ([] for _ in range(8))
    for l in range(depth):
        i = l // 2
        if l % 2 == 0:
            z = norm_matmul(x, norm_mix[l], w_in_all, i, zero_bias, name=f"in_proj{l}")
            pcols = (cols["dq"], cols["dkv"], cols["akr"])
            pargs = (mla_q_norm[i], mla_kv_norm[i], wuq_all, w2_all, i, b_gate_a[i])
            loga_p, qp_p, c_p, krn_p, kpe_p = ab_proj(z, pcols, tabs_p, 0, rp, *pargs, name=f"ab_proj_prompt{l}")
            loga_s, qp_s, c_s, krn_s, _ = ab_proj(z, pcols, tabs_s, rp, rs, *pargs, name=f"ab_proj_sample{l}")
            gcols = (cols["q"], cols["k"], cols["v"], cols["g"])
            a = jnp.zeros((r, gla_v + mla_w), BF16)
            a, sg_p = gla_prompt(z, loga_p, gla_norm[i], gcols, a, bp, s, name=f"gla_prompt{l}")
            a, sg_s = gla_sample(z, loga_s, gla_norm[i], gcols, state_gla, i, a, rp, bd, t, name=f"gla_sample{l}")
            a = mla_self(qp_p, c_p, kpe_p, wkv_all, i, a, gla_v, bp, s, name=f"mla_self{l}")
            q_abs = q_absorb(qp_s, wq_abs_all, i, name=f"q_absorb{l}")
            q_abs = q_abs.reshape(MLA_HEADS, bd, t, -1).transpose(1, 0, 2, 3).reshape(bd, MLA_HEADS * t, -1)
            c_s3 = c_s.reshape(bd, t, kv_lora)
            krn_s3 = krn_s.reshape(bd, t, MLA_ROPE)
            o_lat = mla_cached(page_table, q_abs, c_s3, krn_s3, cache_kv, cache_kr_t, i, name=f"mla_cached{l}")
            o_lat = o_lat.reshape(bd, MLA_HEADS, t, kv_lora).transpose(1, 0, 2, 3).reshape(MLA_HEADS, rs, kv_lora)
            a = v_up(o_lat, wv_pair_all, i, a, rp, gla_v, name=f"v_up{l}")
            x = out_proj_residual(a, w_out_all, i, x, name=f"out_proj{l}")
            kv_p.append(c_p.reshape(bp, s, kv_lora)); kr_p.append(krn_p.reshape(bp, s, MLA_ROPE))
            kv_s.append(c_s3); kr_s.append(krn_s3)
            gla_p.append(sg_p); gla_s.append(sg_s)
        else:
            u = norm_matmul(x, norm_mix[l], w_pw1_all, i, b_pw1[i], glu=True, name=f"pw1_glu{l}")
            cargs = (w_dw[i], b_dw[i], conv_ln_g[i], conv_ln_b[i], w_pw2_all, i, b_pw2[i])
            x, st_p = conv_prompt(u, x, *cargs, bp, s, name=f"conv_prompt{l}")
            x, st_s = conv_sample(u, state_conv[i], x, *cargs, rp, bd, t, name=f"conv_sample{l}")
            conv_p.append(st_p); conv_s.append(st_s)
        if l + 1 < depth:
            x = mlp_residual(x, norm_mlp[l], w_up_all, w_down_all, l, name=f"mlp{l}")
    margs = (x, norm_mlp[depth - 1], w_up_all, w_down_all, depth - 1)
    y_p = mlp_residual(*margs, 0, rp, norm_final, name="mlp_final_prompt").reshape(bp, s, d)
    y_s = mlp_residual(*margs, rp, rs, norm_final, name="mlp_final_sample").reshape(bd, t, d)
    return (y_p, y_s, jnp.stack(kv_p), jnp.stack(kr_p), jnp.stack(gla_p), jnp.stack(conv_p),
            jnp.stack(kv_s), jnp.stack(kr_s), jnp.stack(gla_s), jnp.stack(conv_s))
```

```python
import functools
import math

import jax
import jax.numpy as jnp
from jax import lax
from jax.experimental import pallas as pl
from jax.experimental.pallas import tpu as pltpu

F32 = jnp.float32
BF16 = jnp.bfloat16

GLA_HEADS = 4
GLA_DK = 64
GLA_DV = 128
GLA_RANK = 16
GLA_TAU = 16.0
GLA_CHUNK = 64
GLA_SUB = 16
MLA_HEADS = 8
MLA_NOPE = 64
MLA_ROPE = 32
MLA_V = 64
MLA_SCALE = (MLA_NOPE + MLA_ROPE) ** -0.5
Q_PRESCALE = MLA_SCALE * math.log2(math.e)
ROPE_THETA = 10000.0
EPS = 1e-6

LANES = 128
SUBLANES = 8
HEAD_PAD = 128
ROPE_LO = MLA_NOPE
ROPE_HALF = MLA_ROPE // 2
VMEM_LIMIT = 56 * 1024 * 1024


def _cparams(sem):
    return pltpu.CompilerParams(dimension_semantics=sem, vmem_limit_bytes=VMEM_LIMIT)


def _row_tile(n, target):
    best = None
    for t in range(8, min(n, target) + 1, 8):
        if n % t == 0:
            best = t
    assert best is not None, n
    return best


def _const_spec(shape):
    nd = len(shape)
    return pl.BlockSpec(shape, lambda *_: (0,) * nd, pipeline_mode=pl.Buffered(1))


def _layer_spec(shape, layer):
    nd = len(shape)
    return pl.BlockSpec((None,) + tuple(shape), lambda *_: (layer,) + (0,) * nd, pipeline_mode=pl.Buffered(1))


_ANY = pl.BlockSpec(memory_space=pl.ANY)


def _rms(x, g):
    ms = jnp.mean(x * x, axis=-1, keepdims=True)
    return x * lax.rsqrt(ms + EPS) * g


def _sigmoid(x):
    return 1.0 / (1.0 + jnp.exp(-x))


def _dot(a, b):
    return jnp.dot(a, b, preferred_element_type=F32)


def _dot_nt(a, b):
    return lax.dot_general(a, b, (((1,), (1,)), ((), ())), preferred_element_type=F32)


def _split_row_specs(tm, d, n_first):
    return [pl.BlockSpec((tm, d), lambda i: (jnp.minimum(i, n_first - 1), 0)),
            pl.BlockSpec((tm, d), lambda i: (jnp.maximum(i - n_first, 0), 0))]


def _split_row_tile(first_ref, second_ref, n_first):
    return jnp.where(pl.program_id(0) < n_first, first_ref[...], second_ref[...])


def _row_sources(x, target):
    if isinstance(x, tuple):
        (r0, d), r1 = x[0].shape, x[1].shape[0]
        tm = _row_tile(math.gcd(r0, r1), target)
        return x, r0 + r1, d, tm, r0 // tm
    return (x,), x.shape[0], x.shape[1], _row_tile(x.shape[0], target), None


def _norm_matmul_kernel(*refs, glu, n_first):
    if n_first is None:
        x_ref, g_ref, w_ref, b_ref, o_ref = refs
        x = x_ref[...]
    else:
        xa_ref, xb_ref, g_ref, w_ref, b_ref, o_ref = refs
        x = _split_row_tile(xa_ref, xb_ref, n_first)
    h = _rms(x, g_ref[...]).astype(BF16)
    u = _dot(h, w_ref[...]) + b_ref[...]
    if glu:
        half = u.shape[1] // 2
        u = u[:, :half] * _sigmoid(u[:, half:])
    o_ref[...] = u


def norm_matmul(x, g, w_bf16, layer, bias, *, glu=False, name):
    xs, r, d, tm, n_first = _row_sources(x, 1024)
    n = w_bf16.shape[2]
    n_out = n // 2 if glu else n
    x_specs = [pl.BlockSpec((tm, d), lambda i: (i, 0))] if n_first is None else _split_row_specs(tm, d, n_first)
    return pl.pallas_call(
        functools.partial(_norm_matmul_kernel, glu=glu, n_first=n_first),
        grid=(r // tm,),
        in_specs=x_specs + [
            _const_spec((1, d)),
            _layer_spec((d, n), layer),
            _const_spec((1, n)),
        ],
        out_specs=pl.BlockSpec((tm, n_out), lambda i: (i, 0)),
        out_shape=jax.ShapeDtypeStruct((r, n_out), F32),
        compiler_params=_cparams(("parallel",)),
        name=name,
    )(*xs, g.reshape(1, d), w_bf16, bias.reshape(1, n))


def _mlp_kernel(x_ref, g_ref, wup_ref, wdn_ref, *rest, final):
    if final:
        fg_ref, o_ref, h_ref = rest
    else:
        o_ref, h_ref = rest
    x = x_ref[...]
    h_ref[...] = _rms(x, g_ref[...]).astype(BF16)
    o_ref[...] = x

    nf, tf = wdn_ref.shape[0], wdn_ref.shape[1]

    def body(f, carry):
        c0 = pl.multiple_of(f * tf, tf)
        u = jnp.maximum(_dot(h_ref[...], wup_ref[:, pl.ds(c0, tf)]), 0.0)
        o_ref[...] += _dot((u * u).astype(BF16), wdn_ref[f])
        return carry

    lax.fori_loop(0, nf, body, 0, unroll=2 if nf % 2 == 0 else 1)
    if final:
        o_ref[...] = _rms(o_ref[...], fg_ref[...])


def mlp_residual(x, g, wup, wdn4, layer, row0=0, rows=None, final_g=None, *, name):
    d = x.shape[1]
    rows = x.shape[0] if rows is None else rows
    _, nf, tf, _ = wdn4.shape
    tm = _row_tile(math.gcd(row0, rows) if row0 else rows, 1024)
    off = row0 // tm
    final = final_g is not None
    extra_specs, extra = ([_const_spec((1, d))], [final_g.reshape(1, d)]) if final else ([], [])
    return pl.pallas_call(
        functools.partial(_mlp_kernel, final=final),
        grid=(rows // tm,),
        in_specs=[
            pl.BlockSpec((tm, d), lambda i: (i + off, 0)),
            _const_spec((1, d)),
            _layer_spec((d, nf * tf), layer),
            _layer_spec((nf, tf, d), layer),
        ] + extra_specs,
        out_specs=pl.BlockSpec((tm, d), lambda i: (i, 0)),
        out_shape=jax.ShapeDtypeStruct((rows, d), F32),
        scratch_shapes=[pltpu.VMEM((tm, d), BF16)],
        compiler_params=_cparams(("parallel",)),
        name=name,
    )(x, g.reshape(1, d), wup, wdn4, *extra)


def _out_proj_kernel(a_ref, w_ref, *rest, n_first):
    if n_first is None:
        r_ref, o_ref = rest
        res = r_ref[...]
    else:
        ra_ref, rb_ref, o_ref = rest
        res = _split_row_tile(ra_ref, rb_ref, n_first)
    o_ref[...] = res + _dot(a_ref[...], w_ref[...])


def out_proj_residual(a_bf16, w_bf16, layer, res, *, name):
    k = a_bf16.shape[1]
    rs, r, d, tm, n_first = _row_sources(res, 1024)
    r_specs = [pl.BlockSpec((tm, d), lambda i: (i, 0))] if n_first is None else _split_row_specs(tm, d, n_first)
    return pl.pallas_call(
        functools.partial(_out_proj_kernel, n_first=n_first),
        grid=(r // tm,),
        in_specs=[
            pl.BlockSpec((tm, k), lambda i: (i, 0)),
            _layer_spec((k, d), layer),
        ] + r_specs,
        out_specs=pl.BlockSpec((tm, d), lambda i: (i, 0)),
        out_shape=jax.ShapeDtypeStruct((r, d), F32),
        compiler_params=_cparams(("parallel",)),
        name=name,
    )(a_bf16, w_bf16, *rs)


def _rope_block(x, cos_t, sin_a, sin_b):
    up = pltpu.roll(x, LANES - ROPE_HALF, 1)
    dn = pltpu.roll(x, ROPE_HALF, 1)
    return x * cos_t + up * sin_a + dn * sin_b


def _ab_proj_kernel(dq_ref, dkv_ref, akr_ref, cos_ref, sa_ref, sb_ref, qn_ref, kvn_ref,
                    wuq_ref, w2_ref, b2_ref, loga_ref, q_ref, c_ref, kr_ref, kpe_ref):
    cos_t, sin_a, sin_b = cos_ref[...], sa_ref[...], sb_ref[...]
    cq = _rms(dq_ref[...], qn_ref[...]).astype(BF16)
    qm = _dot(cq, wuq_ref[...])
    for h in range(MLA_HEADS):
        blk = qm[:, h * HEAD_PAD:(h + 1) * HEAD_PAD]
        roped = _rope_block(blk, cos_t, sin_a, sin_b)
        q_ref[:, h * HEAD_PAD:(h + 1) * HEAD_PAD] = (roped * Q_PRESCALE).astype(BF16)
    c_ref[...] = _rms(dkv_ref[...], kvn_ref[...])
    akr = akr_ref[...]
    y = _rope_block(akr, cos_t, sin_a, sin_b)
    lane = lax.broadcasted_iota(jnp.int32, y.shape, 1)
    rope_lane = (lane >= ROPE_LO) & (lane < ROPE_LO + MLA_ROPE)
    kpe_ref[...] = jnp.where(rope_lane, y, 0.0).astype(BF16)
    kr_ref[...] = y[:, ROPE_LO:ROPE_LO + MLA_ROPE]
    xa = _dot(akr.astype(BF16), w2_ref[...]) + b2_ref[...]
    log_sig = jnp.minimum(xa, 0.0) - jnp.log(1.0 + jnp.exp(-jnp.abs(xa)))
    loga_ref[...] = log_sig * (1.0 / GLA_TAU)


def ab_proj(z, cols, tabs, row0, rows, q_norm, kv_norm, wuq_pad, w2_pad, layer, b2, *, name):
    tab_rows = tabs[0].shape[0]
    tm = _row_tile(math.gcd(math.gcd(row0, rows) if row0 else rows, tab_rows), 512)
    off = row0 // tm
    ntab = tab_rows // tm
    dq0, dkv0, akr0 = cols
    ql, kvl = q_norm.shape[0], kv_norm.shape[0]
    nq = wuq_pad.shape[2]
    gw = w2_pad.shape[2]
    out_row = lambda w: pl.BlockSpec((tm, w), lambda i: (i, 0))
    tab = pl.BlockSpec((tm, LANES), lambda i: (i % ntab, 0))
    return pl.pallas_call(
        _ab_proj_kernel,
        grid=(rows // tm,),
        in_specs=[
            pl.BlockSpec((tm, ql), lambda i: (i + off, dq0 // ql)),
            pl.BlockSpec((tm, kvl), lambda i: (i + off, dkv0 // kvl)),
            pl.BlockSpec((tm, LANES), lambda i: (i + off, akr0 // LANES)),
            tab, tab, tab,
            _const_spec((1, ql)), _const_spec((1, kvl)),
            _layer_spec((ql, nq), layer), _layer_spec((LANES, gw), layer), _const_spec((1, gw)),
        ],
        out_specs=[out_row(gw), out_row(nq), out_row(kvl), out_row(MLA_ROPE), out_row(LANES)],
        out_shape=[
            jax.ShapeDtypeStruct((rows, gw), F32),
            jax.ShapeDtypeStruct((rows, nq), BF16),
            jax.ShapeDtypeStruct((rows, kvl), F32),
            jax.ShapeDtypeStruct((rows, MLA_ROPE), F32),
            jax.ShapeDtypeStruct((rows, LANES), BF16),
        ],
        compiler_params=_cparams(("parallel",)),
        name=name,
    )(z, z, z, *tabs, q_norm.reshape(1, ql), kv_norm.reshape(1, kvl), wuq_pad, w2_pad, b2.reshape(1, gw))


def _seg_cumsum(x, rg, group):
    k = 1
    while k < group:
        x = x + jnp.where(rg >= k, pltpu.roll(x, k, 0), 0.0)
        k *= 2
    return x


def _seg_first(x, rg, group):
    y = jnp.where(rg == 0, x, 0.0)
    k = 1
    while k < group:
        y = y + pltpu.roll(y, k, 0)
        k *= 2
    return y


def _seg_last(x, rg, group):
    n = x.shape[0]
    y = jnp.where(rg == group - 1, x, 0.0)
    k = 1
    while k < group:
        y = y + pltpu.roll(y, n - k, 0)
        k *= 2
    return y


def _gla_out(o, gate, gn):
    return _rms(o, gn) * (gate * _sigmoid(gate))


def _gla_prompt_kernel(q_ref, k_ref, v_ref, gate_ref, la_ref, gn_ref, a_hbm, o_ref, sf_ref, s_ref, *, nchunk):
    del a_hbm
    c = GLA_CHUNK
    npair = GLA_HEADS // 2
    j = pl.program_id(1)

    @pl.when(j == 0)
    def _():
        s_ref[...] = jnp.zeros_like(s_ref)

    gn = gn_ref[...]
    row = lax.broadcasted_iota(jnp.int32, (c, LANES), 0)
    lane = lax.broadcasted_iota(jnp.int32, (c, LANES), 1)
    head_lane = [lane < GLA_DK, lane >= GLA_DK]
    rg = row & (GLA_SUB - 1)
    t_i = lax.broadcasted_iota(jnp.int32, (c, c), 0)
    s_i = lax.broadcasted_iota(jnp.int32, (c, c), 1)
    sub_shift = GLA_SUB.bit_length() - 1
    tb, sb = t_i >> sub_shift, s_i >> sub_shift
    m_diag = (tb == sb) & (s_i <= t_i)
    m_next = (tb == sb + 1) & ((sb & 1) == 0)
    m_half = (t_i >= c // 2) & (s_i < c // 2)

    def chunk(ci, carry):
        r0 = pl.multiple_of(ci * c, c)
        for p in range(npair):
            qk = slice(p * LANES, (p + 1) * LANES)
            q = q_ref[pl.ds(r0, c), qk] * (GLA_DK ** -0.5)
            k = k_ref[pl.ds(r0, c), qk]
            b = _seg_cumsum(la_ref[pl.ds(r0, c), qk], row, c)
            r_sub = _seg_first(b, rg, GLA_SUB)
            r_next = pltpu.roll(r_sub, c - GLA_SUB, 0)
            b_half = b[c // 2:c // 2 + 1, :]
            q_sub = q * jnp.exp(b - r_sub)
            k_sub = k * jnp.exp(r_sub - b)
            k_next = k * jnp.exp(jnp.minimum(r_next - b, 0.0))
            q_half = q * jnp.exp(jnp.minimum(b - b_half, 0.0))
            k_half = k * jnp.exp(jnp.minimum(b_half - b, 0.0))
            q_state = q * jnp.exp(b)
            k_t = k.T
            b_t = b.T
            b_last = b_t[:, c - 1:c]
            k_upd = k_t * jnp.exp(b_last - b_t)
            s_decay = jnp.exp(b_last)
            s_pair = s_ref[p * LANES:(p + 1) * LANES, :]
            for h in range(2):
                hm = head_lane[h]
                z = lambda a: jnp.where(hm, a, 0.0)
                attn = (jnp.where(m_diag, _dot_nt(z(q_sub), k_sub), 0.0)
                        + jnp.where(m_next, _dot_nt(z(q_sub), k_next), 0.0)
                        + jnp.where(m_half, _dot_nt(z(q_half), k_half), 0.0))
                vcol = slice((2 * p + h) * GLA_DV, (2 * p + h + 1) * GLA_DV)
                vh = v_ref[pl.ds(r0, c), vcol]
                o = _dot(z(q_state), s_pair) + _dot(attn, vh)
                o_ref[pl.ds(r0, c), vcol] = _gla_out(o, gate_ref[pl.ds(r0, c), vcol], gn).astype(BF16)
                lo, hi = h * GLA_DK, (h + 1) * GLA_DK
                srow = slice(p * LANES + lo, p * LANES + hi)
                s_ref[srow, :] = s_decay[lo:hi] * s_pair[lo:hi] + _dot(k_upd[lo:hi], vh)
        return carry

    lax.fori_loop(0, nchunk, chunk, 0, unroll=8 if nchunk % 8 == 0 else 1)

    @pl.when(j == pl.num_programs(1) - 1)
    def _():
        for h in range(GLA_HEADS):
            sf_ref[0, h] = s_ref[h * GLA_DK:(h + 1) * GLA_DK, :]


def gla_prompt(z, loga, gla_norm, cols, a, bp, s, *, name):
    q0, k0, v0, g0 = cols
    tb = _row_tile(s, 512)
    assert tb % GLA_CHUNK == 0
    nj = s // tb
    qk_w, v_w = GLA_HEADS * GLA_DK, GLA_HEADS * GLA_DV
    rowblk = lambda b, j: b * nj + j
    return pl.pallas_call(
        functools.partial(_gla_prompt_kernel, nchunk=tb // GLA_CHUNK),
        grid=(bp, nj),
        in_specs=[
            pl.BlockSpec((tb, qk_w), lambda b, j: (rowblk(b, j), q0 // qk_w)),
            pl.BlockSpec((tb, qk_w), lambda b, j: (rowblk(b, j), k0 // qk_w)),
            pl.BlockSpec((tb, v_w), lambda b, j: (rowblk(b, j), v0 // v_w)),
            pl.BlockSpec((tb, v_w), lambda b, j: (rowblk(b, j), g0 // v_w)),
            pl.BlockSpec((tb, qk_w), lambda b, j: (rowblk(b, j), 0)),
            pl.BlockSpec((1, GLA_DV), lambda b, j: (0, 0)),
            _ANY,
        ],
        out_specs=[
            pl.BlockSpec((tb, v_w), lambda b, j: (rowblk(b, j), 0)),
            pl.BlockSpec((1, GLA_HEADS, GLA_DK, GLA_DV), lambda b, j: (b, 0, 0, 0)),
        ],
        out_shape=[
            jax.ShapeDtypeStruct(a.shape, a.dtype),
            jax.ShapeDtypeStruct((bp, GLA_HEADS, GLA_DK, GLA_DV), F32),
        ],
        scratch_shapes=[pltpu.VMEM((GLA_HEADS * GLA_DK, GLA_DV), F32)],
        input_output_aliases={6: 0},
        compiler_params=_cparams(("parallel", "arbitrary")),
        name=name,
    )(z, z, z, z, loga, gla_norm.reshape(1, GLA_DV), a)


def _gla_sample_kernel(q_ref, k_ref, v_ref, gate_ref, la_ref, gn_ref, s0_ref, a_hbm, o_ref, sn_ref, *, t, nseq):
    del a_hbm
    n = nseq * t
    npair = GLA_HEADS // 2
    gn = gn_ref[...]
    row = lax.broadcasted_iota(jnp.int32, (n, LANES), 0)
    lane = lax.broadcasted_iota(jnp.int32, (n, LANES), 1)
    head_lane = [lane < GLA_DK, lane >= GLA_DK]
    t_shift, lane_shift = t.bit_length() - 1, LANES.bit_length() - 1
    rg = row & (t - 1)
    t_i = lax.broadcasted_iota(jnp.int32, (n, n), 0)
    s_i = lax.broadcasted_iota(jnp.int32, (n, n), 1)
    m_seq = ((t_i >> t_shift) == (s_i >> t_shift)) & (s_i <= t_i)
    wide = nseq * LANES
    w_row = lax.broadcasted_iota(jnp.int32, (n, wide), 0)
    w_col = lax.broadcasted_iota(jnp.int32, (n, wide), 1)
    own_seq = (w_row >> t_shift) == (w_col >> lane_shift)
    own_last = own_seq & ((w_row & (t - 1)) == t - 1)
    w_head = [(w_col & GLA_DK) == 0, (w_col & GLA_DK) != 0]
    srow_head1 = (lax.broadcasted_iota(jnp.int32, (wide, GLA_DV), 0) & GLA_DK) != 0
    tile = lambda a: jnp.concatenate([a] * nseq, axis=1)

    for p in range(npair):
        qk = slice(p * LANES, (p + 1) * LANES)
        q = q_ref[:, qk] * (GLA_DK ** -0.5)
        k = k_ref[:, qk]
        b = _seg_cumsum(la_ref[:, qk], rg, t)
        r_first = _seg_first(b, rg, t)
        r_last = _seg_last(b, rg, t)
        q_sub = q * jnp.exp(b - r_first)
        k_sub = k * jnp.exp(r_first - b)
        q_state = q * jnp.exp(b)
        k_upd = k * jnp.exp(r_last - b)
        s_all = s0_ref[:, 2 * p:2 * p + 2].reshape(wide, GLA_DV)
        q_wide = jnp.where(own_seq, tile(q_state), 0.0)
        k_wide_t = jnp.where(own_seq, tile(k_upd), 0.0).T
        decay = jnp.exp(jnp.sum(jnp.where(own_last, tile(b), 0.0).T, axis=1, keepdims=True))
        upd = []
        for h in range(2):
            attn = jnp.where(m_seq, _dot_nt(jnp.where(head_lane[h], q_sub, 0.0), k_sub), 0.0)
            vcol = slice((2 * p + h) * GLA_DV, (2 * p + h + 1) * GLA_DV)
            vh = v_ref[:, vcol]
            o = _dot(jnp.where(w_head[h], q_wide, 0.0), s_all) + _dot(attn, vh)
            o_ref[:, vcol] = _gla_out(o, gate_ref[:, vcol], gn).astype(BF16)
            upd.append(_dot(k_wide_t, vh))
        s_new = decay * s_all + jnp.where(srow_head1, upd[1], upd[0])
        sn_ref[:, 2 * p:2 * p + 2] = s_new.reshape(nseq, 2, GLA_DK, GLA_DV)


def gla_sample(z, loga, gla_norm, cols, s0_all, layer, a, row0, bd, t, *, name):
    q0, k0, v0, g0 = cols
    nseq = 16 if bd % 16 == 0 else bd
    n = nseq * t
    assert n % 8 == 0 and row0 % n == 0 and t & (t - 1) == 0
    qk_w, v_w = GLA_HEADS * GLA_DK, GLA_HEADS * GLA_DV
    off = row0 // n
    return pl.pallas_call(
        functools.partial(_gla_sample_kernel, t=t, nseq=nseq),
        grid=(bd // nseq,),
        in_specs=[
            pl.BlockSpec((n, qk_w), lambda i: (i + off, q0 // qk_w)),
            pl.BlockSpec((n, qk_w), lambda i: (i + off, k0 // qk_w)),
            pl.BlockSpec((n, v_w), lambda i: (i + off, v0 // v_w)),
            pl.BlockSpec((n, v_w), lambda i: (i + off, g0 // v_w)),
            pl.BlockSpec((n, qk_w), lambda i: (i, 0)),
            pl.BlockSpec((1, GLA_DV), lambda i: (0, 0)),
            pl.BlockSpec((None, nseq, GLA_HEADS, GLA_DK, GLA_DV), lambda i: (layer, i, 0, 0, 0)),
            _ANY,
        ],
        out_specs=[
            pl.BlockSpec((n, v_w), lambda i: (i + off, 0)),
            pl.BlockSpec((nseq, GLA_HEADS, GLA_DK, GLA_DV), lambda i: (i, 0, 0, 0)),
        ],
        out_shape=[
            jax.ShapeDtypeStruct(a.shape, a.dtype),
            jax.ShapeDtypeStruct((bd, GLA_HEADS, GLA_DK, GLA_DV), F32),
        ],
        input_output_aliases={7: 0},
        compiler_params=_cparams(("parallel",)),
        name=name,
    )(z, z, z, z, loga, gla_norm.reshape(1, GLA_DV), s0_all, a)


def _mla_self_kernel(q_ref, c_ref, kpe_ref, w_ref, a_hbm, o_ref, k_scr, v_scr, *, tq, nq):
    del a_hbm
    i = pl.program_id(2)

    @pl.when(i == 0)
    def _():
        kv = _dot(c_ref[...].astype(BF16), w_ref[0])
        kpe = kpe_ref[...].astype(F32)
        for h in range(2):
            k_scr[h] = (kv[:, h * HEAD_PAD:(h + 1) * HEAD_PAD] + kpe).astype(BF16)
            v_scr[h] = kv[:, (2 + h) * HEAD_PAD:(3 + h) * HEAD_PAD].astype(BF16)

    t_i = lax.broadcasted_iota(jnp.int32, (tq, tq), 0)
    s_i = lax.broadcasted_iota(jnp.int32, (tq, tq), 1)
    causal = s_i <= t_i
    qs = [q_ref[:, h * HEAD_PAD:(h + 1) * HEAD_PAD] for h in range(2)]

    def step(kb, carry, masked):
        r0 = kb * tq
        new = []
        for h in range(2):
            m, l, acc = carry[h]
            s = _dot_nt(qs[h], k_scr[h, r0:r0 + tq, :])
            if masked:
                s = jnp.where(causal, s, -jnp.inf)
            m_new = jnp.maximum(m, jnp.max(s, axis=1, keepdims=True))
            alpha = jnp.exp2(m - m_new)
            p = jnp.exp2(s - m_new)
            l = alpha * l + jnp.sum(p, axis=1, keepdims=True)
            acc = alpha * acc + _dot(p.astype(BF16), v_scr[h, r0:r0 + tq, :])
            new.append((m_new, l, acc))
        return tuple(new)

    init1 = (jnp.full((tq, 1), -jnp.inf, F32), jnp.zeros((tq, 1), F32), jnp.zeros((tq, HEAD_PAD), F32))

    def run(nfull):
        carry = (init1, init1)
        for kb in range(nfull):
            carry = step(kb, carry, False)
        (_, l0, acc0), (_, l1, acc1) = step(nfull, carry, True)
        o_ref[...] = (acc0 / l0 + acc1 / l1).astype(BF16)

    for iq in range(nq):
        pl.when(i == iq)(functools.partial(run, iq))


def mla_self(qp, c_new, kpe, wkv_pair, layer, a, col0, bp, s, *, name):
    tq = _row_tile(s, 512)
    nq = s // tq
    npair = MLA_HEADS // 2
    kvl = c_new.shape[1]
    ow = 2 * MLA_V
    assert col0 % ow == 0
    return pl.pallas_call(
        functools.partial(_mla_self_kernel, tq=tq, nq=nq),
        grid=(bp, npair, nq),
        in_specs=[
            pl.BlockSpec((tq, 2 * HEAD_PAD), lambda b, p, i: (b * nq + i, p)),
            pl.BlockSpec((s, kvl), lambda b, p, i: (b, 0)),
            pl.BlockSpec((s, LANES), lambda b, p, i: (b, 0)),
            pl.BlockSpec((None, 1, kvl, 4 * HEAD_PAD), lambda b, p, i: (layer, p, 0, 0)),
            _ANY,
        ],
        out_specs=pl.BlockSpec((tq, ow), lambda b, p, i: (b * nq + i, col0 // ow + p)),
        out_shape=jax.ShapeDtypeStruct(a.shape, a.dtype),
        scratch_shapes=[pltpu.VMEM((2, s, HEAD_PAD), BF16), pltpu.VMEM((2, s, HEAD_PAD), BF16)],
        input_output_aliases={4: 0},
        compiler_params=_cparams(("parallel", "parallel", "arbitrary")),
        name=name,
    )(qp, c_new, kpe, wkv_pair, a)


def _q_absorb_kernel(q_ref, w_ref, o_ref):
    o_ref[0] = _dot(q_ref[...], w_ref[0])


def q_absorb(qp, wq_abs, layer, *, name):
    rows = qp.shape[0]
    wout = wq_abs.shape[3]
    return pl.pallas_call(
        _q_absorb_kernel,
        grid=(MLA_HEADS,),
        in_specs=[
            pl.BlockSpec((rows, HEAD_PAD), lambda h: (0, h)),
            pl.BlockSpec((None, 1, HEAD_PAD, wout), lambda h: (layer, h, 0, 0)),
        ],
        out_specs=pl.BlockSpec((1, rows, wout), lambda h: (h, 0, 0)),
        out_shape=jax.ShapeDtypeStruct((MLA_HEADS, rows, wout), F32),
        compiler_params=_cparams(("parallel",)),
        name=name,
    )(qp, wq_abs)


def _mla_cached_kernel(pt_ref, q_ref, cn_ref, krn_ref, ckv_hbm, ckr_hbm, o_ref, kv_buf, kr_buf, sem,
                       *, layer, npages, page, t, kvl, nchunk, group):
    b = pl.program_id(0)
    nb = pl.num_programs(0)
    slot = b % 2
    cw = npages // nchunk * page

    def copies(step, sl, u, p):
        pg = pt_ref[step * group + u, p]
        rows = pl.ds(p * page, page)
        return (pltpu.make_async_copy(ckv_hbm.at[layer, pg], kv_buf.at[sl, u, rows], sem.at[0, sl]),
                pltpu.make_async_copy(ckr_hbm.at[layer, pg], kr_buf.at[sl, u, :, rows], sem.at[1, sl]))

    def start_all(step, sl):
        for u in range(group):
            for p in range(npages):
                for k, cp in enumerate(copies(step, sl, u, p)):
                    cp.start(priority=(p + k) % 2)

    @pl.when(b == 0)
    def _():
        start_all(0, 0)

    @pl.when(b + 1 < nb)
    def _():
        start_all(b + 1, 1 - slot)

    n = q_ref.shape[1]
    tok = lax.broadcasted_iota(jnp.int32, (n, 1), 0) & (t - 1)
    state = []
    for u in range(group):
        q = q_ref[u]
        q_lat = q[:, :kvl]
        q_pe = q[:, kvl:kvl + MLA_ROPE]
        c_new = cn_ref[u]
        kr_new = krn_ref[u]
        s_new = []
        for jn in range(t):
            sj = (jnp.sum(q_lat * c_new[jn:jn + 1, :], axis=1, keepdims=True)
                  + jnp.sum(q_pe * kr_new[jn:jn + 1, :], axis=1, keepdims=True))
            s_new.append(jnp.where(tok >= jn, sj, -jnp.inf))
        m = s_new[0]
        for sj in s_new[1:]:
            m = jnp.maximum(m, sj)
        l = jnp.zeros((n, 1), F32)
        o = jnp.zeros((n, kvl), F32)
        for sj, jn in zip(s_new, range(t)):
            pj = jnp.exp2(sj - m)
            l = l + pj
            o = o + pj * c_new[jn:jn + 1, :]
        q_rows = jnp.concatenate([q_lat.astype(BF16), jnp.zeros((LANES - n, kvl), BF16)], axis=0)
        state.append((m, l, o, q_rows, q_pe))
    for u in range(group):
        for p in range(npages):
            for cp in copies(b, slot, u, p):
                cp.wait()
    for ch in range(nchunk):
        for u in range(group):
            m, l, o, q_rows, q_pe = state[u]
            past_c = kv_buf[slot, u, ch * cw:(ch + 1) * cw, :].astype(BF16)
            past_rt = kr_buf[slot, u, :, ch * cw:(ch + 1) * cw]
            s = _dot_nt(past_c, q_rows).T[:n] + _dot(q_pe, past_rt)
            m_new = jnp.maximum(m, jnp.max(s, axis=1, keepdims=True))
            alpha = jnp.exp2(m - m_new)
            p_c = jnp.exp2(s - m_new)
            l = alpha * l + jnp.sum(p_c, axis=1, keepdims=True)
            o = alpha * o + _dot(p_c.astype(BF16), past_c)
            state[u] = (m_new, l, o, q_rows, q_pe)
    for u in range(group):
        _, l, o, _, _ = state[u]
        o_ref[u] = o / l


def mla_cached(page_table, q_abs, c_new_s, kr_new_s, cache_kv, cache_kr_t, layer, *, name):
    bd, npages = page_table.shape
    page, kvl = cache_kv.shape[2], cache_kv.shape[3]
    rope = cache_kr_t.shape[2]
    t = c_new_s.shape[1]
    n, qw = q_abs.shape[1], q_abs.shape[2]
    past = npages * page
    nchunk = 4 if npages % 4 == 0 else 1
    group = 1
    assert t & (t - 1) == 0 and n <= LANES
    grid_spec = pltpu.PrefetchScalarGridSpec(
        num_scalar_prefetch=1,
        grid=(bd // group,),
        in_specs=[
            pl.BlockSpec((group, n, qw), lambda b, pt: (b, 0, 0)),
            pl.BlockSpec((group, t, kvl), lambda b, pt: (b, 0, 0)),
            pl.BlockSpec((group, t, rope), lambda b, pt: (b, 0, 0)),
            _ANY,
            _ANY,
        ],
        out_specs=pl.BlockSpec((group, n, kvl), lambda b, pt: (b, 0, 0)),
        scratch_shapes=[
            pltpu.VMEM((2, group, past, kvl), F32),
            pltpu.VMEM((2, group, rope, past), F32),
            pltpu.SemaphoreType.DMA((2, 2)),
        ],
    )
    return pl.pallas_call(
        functools.partial(_mla_cached_kernel, layer=layer, npages=npages, page=page, t=t, kvl=kvl,
                          nchunk=nchunk, group=group),
        grid_spec=grid_spec,
        out_shape=jax.ShapeDtypeStruct((bd, n, kvl), F32),
        compiler_params=_cparams(("arbitrary",)),
        name=name,
    )(page_table, q_abs, c_new_s, kr_new_s, cache_kv, cache_kr_t)


def _v_up_kernel(o_ref, w_ref, a_hbm, out_ref):
    del a_hbm
    out_ref[...] = (_dot(o_ref[0].astype(BF16), w_ref[0, 0]) + _dot(o_ref[1].astype(BF16), w_ref[0, 1])).astype(BF16)


def v_up(o_lat, wv_pair, layer, a, row0, col0, *, name):
    h, rows, kvl = o_lat.shape
    ow = 2 * MLA_V
    assert row0 % rows == 0 and col0 % ow == 0
    return pl.pallas_call(
        _v_up_kernel,
        grid=(h // 2,),
        in_specs=[
            pl.BlockSpec((2, rows, kvl), lambda p: (p, 0, 0)),
            pl.BlockSpec((None, 1, 2, kvl, ow), lambda p: (layer, p, 0, 0, 0)),
            _ANY,
        ],
        out_specs=pl.BlockSpec((rows, ow), lambda p: (row0 // rows, col0 // ow + p)),
        out_shape=jax.ShapeDtypeStruct(a.shape, a.dtype),
        input_output_aliases={2: 0},
        compiler_params=_cparams(("parallel",)),
        name=name,
    )(o_lat, wv_pair, a)


CONV_PAD = 32
CONV_SUB = 64
CONV_LANES = 256


def _ln_swish(y, lg, lb):
    mu = jnp.mean(y, axis=-1, keepdims=True)
    yc = y - mu
    var = jnp.mean(yc * yc, axis=-1, keepdims=True)
    yn = yc * lax.rsqrt(var + EPS) * lg + lb
    return yn * _sigmoid(yn)


def _conv_prompt_kernel(u_ref, halo_ref, x_ref, wdw_ref, bdw_ref, lg_ref, lb_ref, w2_ref, b2_ref,
                        o_ref, st_ref, ext_ref, sh_ref, h_ref, *, tt, cw):
    hist = cw - 1
    lo = CONV_PAD - hist
    d = u_ref.shape[1]
    j = pl.program_id(1)

    @pl.when(j == 0)
    def _():
        ext_ref[0:CONV_PAD, :] = jnp.zeros((CONV_PAD, d), F32)

    @pl.when(j > 0)
    def _():
        ext_ref[0:CONV_PAD, :] = halo_ref[...]

    ext_ref[CONV_PAD:CONV_PAD + tt, :] = u_ref[...]
    st_ref[0] = ext_ref[CONV_PAD + tt - hist:CONV_PAD + tt, :]
    span = tt + CONV_PAD - SUBLANES
    for r in range(1, SUBLANES):
        for c0 in range(0, d, CONV_LANES):
            sh_ref[r - 1, 0:span, c0:c0 + CONV_LANES] = ext_ref[r:r + span, c0:c0 + CONV_LANES]

    grp = (CONV_SUB // SUBLANES, SUBLANES, LANES)
    for c0 in range(0, d, LANES):
        cs = slice(c0, c0 + LANES)
        wts = [wdw_ref[w, :, cs] for w in range(cw)]
        bias = jnp.broadcast_to(bdw_ref[:, cs].reshape(1, 1, LANES), grp)

        for r0 in range(0, tt, CONV_SUB):
            acc = bias
            for w in range(cw):
                r, a = (w + lo) % SUBLANES, (w + lo) // SUBLANES
                rows = slice(r0 + a * SUBLANES, r0 + a * SUBLANES + CONV_SUB)
                src = ext_ref[rows, cs] if r == 0 else sh_ref[r - 1, rows, cs]
                acc = acc + src.reshape(grp) * wts[w][None]
            h_ref[r0:r0 + CONV_SUB, cs] = acc.reshape(CONV_SUB, LANES)

    for r0 in range(0, tt, CONV_SUB):
        h_ref[r0:r0 + CONV_SUB, :] = _ln_swish(h_ref[r0:r0 + CONV_SUB, :], lg_ref[...], lb_ref[...])
    o_ref[...] = x_ref[...] + b2_ref[...] + _dot(h_ref[...].astype(BF16), w2_ref[...])


def conv_prompt(u, x, w_dw, b_dw, ln_g, ln_b, w2_bf16, layer, b2, bp, s, *, name):
    r, d = x.shape
    cw = w_dw.shape[0]
    hist = cw - 1
    tt = _row_tile(s, 512)
    assert hist <= CONV_PAD and tt % CONV_SUB == 0 and tt % CONV_PAD == 0 and d % CONV_LANES == 0
    nj = s // tt
    hb = tt // CONV_PAD
    blk = pl.BlockSpec((tt, d), lambda b, j: (b * nj + j, 0))
    return pl.pallas_call(
        functools.partial(_conv_prompt_kernel, tt=tt, cw=cw),
        grid=(bp, nj),
        in_specs=[
            blk,
            pl.BlockSpec((CONV_PAD, d), lambda b, j: (jnp.maximum((b * nj + j) * hb - 1, 0), 0)),
            blk,
            _const_spec((cw, SUBLANES, d)), _const_spec((1, d)), _const_spec((1, d)), _const_spec((1, d)),
            _layer_spec((d, d), layer), _const_spec((1, d)),
        ],
        out_specs=[blk, pl.BlockSpec((1, hist, d), lambda b, j: (b, 0, 0))],
        out_shape=[jax.ShapeDtypeStruct((r, d), F32), jax.ShapeDtypeStruct((bp, hist, d), F32)],
        scratch_shapes=[
            pltpu.VMEM((tt + CONV_PAD, d), F32),
            pltpu.VMEM((SUBLANES - 1, tt + CONV_PAD, d), F32),
            pltpu.VMEM((tt, d), F32),
        ],
        input_output_aliases={2: 0},
        compiler_params=_cparams(("parallel", "arbitrary")),
        name=name,
    )(u, u, x, jnp.broadcast_to(w_dw[:, None, :], (cw, SUBLANES, d)), b_dw.reshape(1, d), ln_g.reshape(1, d),
      ln_b.reshape(1, d), w2_bf16, b2.reshape(1, d))


def _conv_sample_kernel(u_ref, buf_ref, x_ref, wdw_ref, bdw_ref, lg_ref, lb_ref, w2_ref, b2_ref,
                        o_ref, st_ref, ext_ref, h_ref, *, t, bb, cw):
    hist = cw - 1
    lo = CONV_PAD - hist
    for i in range(bb):
        ext_ref[i, lo:CONV_PAD, :] = buf_ref[i]
        ext_ref[i, CONV_PAD:CONV_PAD + t, :] = u_ref[i * t:(i + 1) * t, :]
        st_ref[i] = ext_ref[i, CONV_PAD + t - hist:CONV_PAD + t, :]
        acc = jnp.broadcast_to(bdw_ref[...], (t, bdw_ref.shape[1]))
        for w in range(cw):
            acc = acc + ext_ref[i, lo + w:lo + w + t, :] * wdw_ref[w:w + 1, :]
        h_ref[i * t:(i + 1) * t, :] = _ln_swish(acc, lg_ref[...], lb_ref[...])
    o_ref[...] = x_ref[...] + b2_ref[...] + _dot(h_ref[...].astype(BF16), w2_ref[...])


def conv_sample(u, buf, x, w_dw, b_dw, ln_g, ln_b, w2_bf16, layer, b2, row0, bd, t, *, name):
    r, d = x.shape
    cw = w_dw.shape[0]
    hist = cw - 1
    bb = 32 if bd % 32 == 0 else bd
    rows = bb * t
    assert hist <= CONV_PAD and rows % 8 == 0 and row0 % rows == 0
    off = row0 // rows
    blk = pl.BlockSpec((rows, d), lambda i: (i + off, 0))
    ext_rows = -(-(CONV_PAD + t) // SUBLANES) * SUBLANES
    return pl.pallas_call(
        functools.partial(_conv_sample_kernel, t=t, bb=bb, cw=cw),
        grid=(bd // bb,),
        in_specs=[
            blk,
            pl.BlockSpec((bb, hist, d), lambda i: (i, 0, 0)),
            blk,
            _const_spec((cw, d)), _const_spec((1, d)), _const_spec((1, d)), _const_spec((1, d)),
            _layer_spec((d, d), layer), _const_spec((1, d)),
        ],
        out_specs=[blk, pl.BlockSpec((bb, hist, d), lambda i: (i, 0, 0))],
        out_shape=[jax.ShapeDtypeStruct((r, d), F32), jax.ShapeDtypeStruct((bd, hist, d), F32)],
        scratch_shapes=[pltpu.VMEM((bb, ext_rows, d), F32), pltpu.VMEM((rows, d), F32)],
        input_output_aliases={2: 0},
        compiler_params=_cparams(("parallel",)),
        name=name,
    )(u, buf, x, w_dw, b_dw.reshape(1, d), ln_g.reshape(1, d), ln_b.reshape(1, d), w2_bf16, b2.reshape(1, d))


def _rope_tables(pos):
    inv = jnp.power(ROPE_THETA, -jnp.arange(0, MLA_ROPE, 2, dtype=F32) / MLA_ROPE)
    ang = pos.astype(F32)[:, None] * inv[None, :]
    cos, sin = jnp.cos(ang), jnp.sin(ang)
    n = pos.shape[0]
    lo = ROPE_LO
    cos_t = jnp.ones((n, LANES), F32).at[:, lo:lo + MLA_ROPE].set(jnp.concatenate([cos, cos], axis=1))
    sin_a = jnp.zeros((n, LANES), F32).at[:, lo:lo + ROPE_HALF].set(-sin)
    sin_b = jnp.zeros((n, LANES), F32).at[:, lo + ROPE_HALF:lo + MLA_ROPE].set(sin)
    return cos_t, sin_a, sin_b


def _layout_w_in(w_in, gla_qk, gla_v, q_lora, kv_lora):
    nl, d = w_in.shape[0], w_in.shape[1]
    sp = [0, gla_qk, 2 * gla_qk, 2 * gla_qk + gla_v, 2 * gla_qk + gla_v + GLA_RANK]
    sp += [sp[-1] + gla_v, sp[-1] + gla_v + q_lora, sp[-1] + gla_v + q_lora + kv_lora]
    q, k, v, a, g, dq, dkv, kr = [w_in[..., lo:hi] for lo, hi in zip(sp, sp[1:] + [w_in.shape[2]])]
    last = jnp.zeros((nl, d, LANES), w_in.dtype).at[..., :GLA_RANK].set(a)
    last = last.at[..., ROPE_LO:ROPE_LO + MLA_ROPE].set(kr)
    w = jnp.concatenate([q, k, v, g, dq, dkv, last], axis=2)
    cols = dict(q=0, k=gla_qk, v=2 * gla_qk, g=2 * gla_qk + gla_v, dq=2 * gla_qk + 2 * gla_v)
    cols["dkv"] = cols["dq"] + q_lora
    cols["akr"] = cols["dkv"] + kv_lora
    return w.astype(BF16), cols


def _layout_w_uq(w_uq):
    ql = w_uq.shape[0]
    w = w_uq.reshape(ql, MLA_HEADS, MLA_NOPE + MLA_ROPE)
    w = jnp.pad(w, ((0, 0), (0, 0), (0, HEAD_PAD - MLA_NOPE - MLA_ROPE)))
    return w.reshape(ql, MLA_HEADS * HEAD_PAD).astype(BF16)


def _layout_w_ukv(w_ukv):
    kvl = w_ukv.shape[0]
    w = w_ukv.reshape(kvl, MLA_HEADS, MLA_NOPE + MLA_V)
    wk = jnp.pad(w[..., :MLA_NOPE], ((0, 0), (0, 0), (0, HEAD_PAD - MLA_NOPE)))
    wv = w[..., MLA_NOPE:]
    wv_e = jnp.pad(wv[:, 0::2], ((0, 0), (0, 0), (0, MLA_V)))
    wv_o = jnp.pad(wv[:, 1::2], ((0, 0), (0, 0), (MLA_V, 0)))
    pair = jnp.stack([wk[:, 0::2], wk[:, 1::2], wv_e, wv_o], axis=2)
    pair = pair.transpose(1, 0, 2, 3).reshape(MLA_HEADS // 2, kvl, 4 * HEAD_PAD)
    w_uk_t = w[..., :MLA_NOPE].transpose(1, 2, 0)
    wq_abs = jnp.zeros((MLA_HEADS, HEAD_PAD, kvl + LANES), F32)
    wq_abs = wq_abs.at[:, :MLA_NOPE, :kvl].set(w_uk_t)
    wq_abs = wq_abs.at[:, ROPE_LO:ROPE_LO + MLA_ROPE, kvl:kvl + MLA_ROPE].set(jnp.eye(MLA_ROPE, dtype=F32))
    wv_h = wv.transpose(1, 0, 2)
    wv_pair = jnp.stack([jnp.pad(wv_h[0::2], ((0, 0), (0, 0), (0, MLA_V))),
                         jnp.pad(wv_h[1::2], ((0, 0), (0, 0), (MLA_V, 0)))], axis=1)
    return pair.astype(BF16), wq_abs.astype(BF16), wv_pair.astype(BF16)


def kernel(x_prompt, x_sample, cache_kv, cache_kr, state_gla, state_conv, page_table, norm_mix, norm_mlp,
           norm_final, w_in, w_gate_a2, b_gate_a, gla_norm, mla_q_norm, mla_kv_norm, w_uq, w_ukv, w_out_ab,
           w_pw1, b_pw1, w_dw, b_dw, conv_ln_g, conv_ln_b, w_pw2, b_pw2, w_up, w_down):
    bp, s, d = x_prompt.shape
    bd, t, _ = x_sample.shape
    rp, rs = bp * s, bd * t
    r = rp + rs
    depth = norm_mix.shape[0]
    page = cache_kv.shape[2]
    past = page_table.shape[1] * page
    gla_qk, gla_v = GLA_HEADS * GLA_DK, GLA_HEADS * GLA_DV
    mla_w = MLA_HEADS * MLA_V
    q_lora, kv_lora = mla_q_norm.shape[1], mla_kv_norm.shape[1]
    d_ff = w_up.shape[2]
    tf = 512 if d_ff % 512 == 0 else d_ff

    tabs_p = _rope_tables(jnp.arange(s))
    tabs_s = _rope_tables(jnp.tile(past + jnp.arange(t), bd))
    cache_kr_t = jnp.swapaxes(cache_kr, 2, 3)

    w_in_all, cols = _layout_w_in(w_in, gla_qk, gla_v, q_lora, kv_lora)
    zero_bias = jnp.zeros((w_in_all.shape[2],), F32)
    w_pw1_all, w_pw2_all, w_out_all = w_pw1.astype(BF16), w_pw2.astype(BF16), w_out_ab.astype(BF16)
    w_up_all = w_up.astype(BF16)
    wuq_all = jax.vmap(_layout_w_uq)(w_uq)
    wkv_all, wq_abs_all, wv_pair_all = jax.vmap(_layout_w_ukv)(w_ukv)
    w2_all = jnp.zeros((w_gate_a2.shape[0], LANES, gla_qk), F32).at[:, :GLA_RANK].set(w_gate_a2).astype(BF16)
    w_down_all = w_down.astype(BF16).reshape(depth, d_ff // tf, tf, d)

    x = (x_prompt.reshape(rp, d), x_sample.reshape(rs, d))
    kv_p, kr_p, gla_p, conv_p, kv_s, kr_s, gla_s, conv_s = ([] for _ in range(8))
    for l in range(depth):
        i = l // 2
        if l % 2 == 0:
            z = norm_matmul(x, norm_mix[l], w_in_all, i, zero_bias, name=f"in_proj{l}")
            pcols = (cols["dq"], cols["dkv"], cols["akr"])
            pargs = (mla_q_norm[i], mla_kv_norm[i], wuq_all, w2_all, i, b_gate_a[i])
            loga_p, qp_p, c_p, krn_p, kpe_p = ab_proj(z, pcols, tabs_p, 0, rp, *pargs, name=f"ab_proj_prompt{l}")
            loga_s, qp_s, c_s, krn_s, _ = ab_proj(z, pcols, tabs_s, rp, rs, *pargs, name=f"ab_proj_sample{l}")
            gcols = (cols["q"], cols["k"], cols["v"], cols["g"])
            a = jnp.zeros((r, gla_v + mla_w), BF16)
            a, sg_p = gla_prompt(z, loga_p, gla_norm[i], gcols, a, bp, s, name=f"gla_prompt{l}")
            a, sg_s = gla_sample(z, loga_s, gla_norm[i], gcols, state_gla, i, a, rp, bd, t, name=f"gla_sample{l}")
            a = mla_self(qp_p, c_p, kpe_p, wkv_all, i, a, gla_v, bp, s, name=f"mla_self{l}")
            q_abs = q_absorb(qp_s, wq_abs_all, i, name=f"q_absorb{l}")
            q_abs = q_abs.reshape(MLA_HEADS, bd, t, -1).transpose(1, 0, 2, 3).reshape(bd, MLA_HEADS * t, -1)
            c_s3 = c_s.reshape(bd, t, kv_lora)
            krn_s3 = krn_s.reshape(bd, t, MLA_ROPE)
            o_lat = mla_cached(page_table, q_abs, c_s3, krn_s3, cache_kv, cache_kr_t, i, name=f"mla_cached{l}")
            o_lat = o_lat.reshape(bd, MLA_HEADS, t, kv_lora).transpose(1, 0, 2, 3).reshape(MLA_HEADS, rs, kv_lora)
            a = v_up(o_lat, wv_pair_all, i, a, rp, gla_v, name=f"v_up{l}")
            x = out_proj_residual(a, w_out_all, i, x, name=f"out_proj{l}")
            kv_p.append(c_p.reshape(bp, s, kv_lora)); kr_p.append(krn_p.reshape(bp, s, MLA_ROPE))
            kv_s.append(c_s3); kr_s.append(krn_s3)
            gla_p.append(sg_p); gla_s.append(sg_s)
        else:
            u = norm_matmul(x, norm_mix[l], w_pw1_all, i, b_pw1[i], glu=True, name=f"pw1_glu{l}")
            cargs = (w_dw[i], b_dw[i], conv_ln_g[i], conv_ln_b[i], w_pw2_all, i, b_pw2[i])
            x, st_p = conv_prompt(u, x, *cargs, bp, s, name=f"conv_prompt{l}")
            x, st_s = conv_sample(u, state_conv[i], x, *cargs, rp, bd, t, name=f"conv_sample{l}")
            conv_p.append(st_p); conv_s.append(st_s)
        if l + 1 < depth:
            x = mlp_residual(x, norm_mlp[l], w_up_all, w_down_all, l, name=f"mlp{l}")
    margs = (x, norm_mlp[depth - 1], w_up_all, w_down_all, depth - 1)
    y_p = mlp_residual(*margs, 0, rp, norm_final, name="mlp_final_prompt").reshape(bp, s, d)
    y_s = mlp_residual(*margs, rp, rs, norm_final, name="mlp_final_sample").reshape(bd, t, d)
    return (y_p, y_s, jnp.stack(kv_p), jnp.stack(kr_p), jnp.stack(gla_p), jnp.stack(conv_p),
            jnp.stack(kv_s), jnp.stack(kr_s), jnp.stack(gla_s), jnp.stack(conv_s))
```
